```python
import jax, jax.numpy as jnp
from jax import lax
import numpy as np

D_MODEL = 2048
BATCH = 2
SEQ = 4096
DEPTH = 2
DEC_BATCH = 128
DEC_SEQ = 1
PAST_LEN = 2048
PAGE_SIZE = 128

HEAD_DIM = 128
MIX_WIDTH = D_MODEL
N_RET_HEADS = MIX_WIDTH // (2 * HEAD_DIM)
N_FOX_HEADS = MIX_WIDTH // (2 * HEAD_DIM)
RET_WIDTH = N_RET_HEADS * HEAD_DIM
FOX_WIDTH = N_FOX_HEADS * HEAD_DIM
IN_WIDTH = 4 * RET_WIDTH + 3 * FOX_WIDTH + N_FOX_HEADS
IN_SPLITS = (RET_WIDTH, 2 * RET_WIDTH, 3 * RET_WIDTH, 4 * RET_WIDTH,
             4 * RET_WIDTH + FOX_WIDTH, 4 * RET_WIDTH + 2 * FOX_WIDTH, 4 * RET_WIDTH + 3 * FOX_WIDTH)
RET_CHUNK = 128
FOX_BLOCK = 128
ROPE_BASE = 10000.0
N_EXPERTS = 16
N_EXPERT_GROUPS = 4
EXPERTS_PER_GROUP = N_EXPERTS // N_EXPERT_GROUPS
TOP_K = 2
D_EXPERT = D_MODEL // 4
N_MOD = 6
EPS = 1e-6
FORGET_BIAS_INIT = 3.0

kernel_name = 'hymba_retention_fox_grouped_moe_step'


def rms_norm(x, g):
    xf = x.astype(jnp.float32)
    y = xf * lax.rsqrt(jnp.mean(xf * xf, axis=-1, keepdims=True) + EPS)
    return (y * g.astype(jnp.float32)).astype(x.dtype)


def split_heads(t):
    return t.reshape(t.shape[:-1] + (-1, HEAD_DIM))


def rope(x, pos):
    half = HEAD_DIM // 2
    inv_freq = ROPE_BASE ** (-jnp.arange(half, dtype=jnp.float32) / half)
    ang = pos.astype(jnp.float32)[:, None] * inv_freq[None, :]
    cos = jnp.cos(ang)[:, None, :]
    sin = jnp.sin(ang)[:, None, :]
    xf = x.astype(jnp.float32)
    x1, x2 = xf[..., :half], xf[..., half:]
    return jnp.concatenate([x1 * cos - x2 * sin, x1 * sin + x2 * cos], axis=-1).astype(x.dtype)


def adaln(c, w_ada, b_ada):
    mod = (jax.nn.silu(c) @ w_ada + b_ada)[:, None, :]
    return jnp.split(mod, N_MOD, axis=-1)


def retention_log_decay():
    return jnp.log1p(-jnp.exp2(-5.0 - jnp.arange(N_RET_HEADS, dtype=jnp.float32)))


def retention_chunk(state, q, k, v):
    C = q.shape[2]
    dt = q.dtype
    lg = retention_log_decay()[:, None]
    t = jnp.arange(C, dtype=jnp.float32)[None, :]
    diff = t[:, :, None] - t[:, None, :]
    intra_decay = jnp.where(diff >= 0, jnp.exp(lg[:, :, None] * jnp.maximum(diff, 0.0)), 0.0)
    scores = jnp.einsum('bhtd,bhsd->bhts', q, k) * intra_decay.astype(dt)
    intra = jnp.einsum('bhts,bhse->bhte', scores, v)
    cross = jnp.einsum('bhtd,bhde->bhte', q, state) * jnp.exp(lg * (t + 1.0)).astype(dt)[None, :, :, None]
    k_dec = k * jnp.exp(lg * (C - 1.0 - t)).astype(dt)[None, :, :, None]
    new_state = (state * jnp.exp(lg * C).astype(dt)[None, :, :, None]
                 + jnp.einsum('bhsd,bhse->bhde', k_dec, v))
    return new_state, intra + cross


def retention_prompt(q, k, v):
    B, S, H, D = q.shape
    n = S // RET_CHUNK

    def to_chunks(a):
        return a.transpose(0, 2, 1, 3).reshape(B, H, n, RET_CHUNK, D).transpose(2, 0, 1, 3, 4)

    state0 = jnp.zeros((B, H, D, D), q.dtype)

    def step(state, qkv):
        return retention_chunk(state, qkv[0], qkv[1], qkv[2])

    s_fin, o = lax.scan(step, state0, (to_chunks(q), to_chunks(k), to_chunks(v)))
    o = o.transpose(1, 0, 3, 2, 4).reshape(B, S, H, D)
    return o, s_fin


def fox_prompt(q, k, v, logf):
    B, S, H, D = q.shape
    scale = HEAD_DIM ** -0.5
    F = jnp.cumsum(logf, axis=1).transpose(0, 2, 1)
    kpos = jnp.arange(S)

    def block(i):
        start = i * FOX_BLOCK
        qb = lax.dynamic_slice_in_dim(q, start, FOX_BLOCK, axis=1)
        Fq = lax.dynamic_slice_in_dim(F, start, FOX_BLOCK, axis=2)
        s = (jnp.einsum('bqhd,bkhd->bhqk', qb, k).astype(jnp.float32) * scale
             + Fq[..., :, None] - F[..., None, :])
        qpos = start + jnp.arange(FOX_BLOCK)
        s = jnp.where(kpos[None, :] <= qpos[:, None], s, -jnp.inf)
        p = jax.nn.softmax(s, axis=-1).astype(v.dtype)
        return jnp.einsum('bhqk,bkhd->bqhd', p, v)

    out = lax.map(block, jnp.arange(S // FOX_BLOCK))
    return jnp.moveaxis(out, 0, 1).reshape(B, S, H, D)


def fox_sample(q, k_new, v_new, logf_new, k_past, v_past, logf_past):
    T = q.shape[1]
    P = k_past.shape[1]
    scale = HEAD_DIM ** -0.5
    F_past = jnp.cumsum(logf_past.astype(jnp.float32), axis=1)
    F_new = F_past[:, -1:] + jnp.cumsum(logf_new.astype(jnp.float32), axis=1)
    Fp = F_past.transpose(0, 2, 1)
    Fn = F_new.transpose(0, 2, 1)
    s_past = (jnp.einsum('bqhd,bkhd->bhqk', q, k_past).astype(jnp.float32) * scale
              + Fn[..., :, None] - Fp[..., None, :])
    s_new = (jnp.einsum('bqhd,bkhd->bhqk', q, k_new).astype(jnp.float32) * scale
             + Fn[..., :, None] - Fn[..., None, :])
    s_new = jnp.where(jnp.tril(jnp.ones((T, T), dtype=bool)), s_new, -jnp.inf)
    p = jax.nn.softmax(jnp.concatenate([s_past, s_new], axis=-1), axis=-1).astype(v_new.dtype)
    return (jnp.einsum('bhqk,bkhd->bqhd', p[..., :P], v_past)
            + jnp.einsum('bhqk,bkhd->bqhd', p[..., P:], v_new))


def mixer_in(x, shift, scale, norm_g, w_in, b_forget, q_norm_g, k_norm_g, pos):
    h = rms_norm(x, norm_g) * (1 + scale) + shift
    z = h @ w_in
    rq, rk, rv, rg, fq, fk, fv, ff = jnp.split(z, IN_SPLITS, axis=-1)
    rq = rope(split_heads(rq), pos)
    rk = rope(split_heads(rk), pos) * (HEAD_DIM ** -0.5)
    fq = rms_norm(split_heads(fq), q_norm_g)
    fk = rms_norm(split_heads(fk), k_norm_g)
    logf = jax.nn.log_sigmoid((ff + b_forget).astype(jnp.float32))
    return rq, rk, split_heads(rv), rg, fq, fk, split_heads(fv), logf


def mixer_out(ret_o, rg, ret_norm_g, fox_o, w_out):
    B, T = ret_o.shape[:2]
    of = ret_o.astype(jnp.float32)
    mu = jnp.mean(of, axis=-1, keepdims=True)
    var = jnp.mean(jnp.square(of - mu), axis=-1, keepdims=True)
    gn = ((of - mu) * lax.rsqrt(var + EPS)).reshape(B, T, RET_WIDTH) * ret_norm_g.astype(jnp.float32)
    ret = gn.astype(ret_o.dtype) * jax.nn.silu(rg)
    mix = jnp.concatenate([ret, fox_o.reshape(B, T, FOX_WIDTH)], axis=-1)
    return mix @ w_out


def route(h, w_router, b_router):
    scores = jax.nn.sigmoid((h @ w_router).astype(jnp.float32))
    biased = scores + b_router.astype(jnp.float32)
    grouped = biased.reshape(-1, N_EXPERT_GROUPS, EXPERTS_PER_GROUP)
    group_score = jnp.sum(lax.top_k(grouped, 2)[0], axis=-1)
    gmask = jax.nn.one_hot(jnp.argmax(group_score, axis=-1), N_EXPERT_GROUPS)
    emask = jnp.repeat(gmask, EXPERTS_PER_GROUP, axis=-1) > 0
    _, idx = lax.top_k(jnp.where(emask, biased, -jnp.inf), TOP_K)
    sel = jnp.take_along_axis(scores, idx, axis=-1)
    w = sel / jnp.sum(sel, axis=-1, keepdims=True)
    return jnp.sum(jax.nn.one_hot(idx, N_EXPERTS) * w[..., None], axis=1)


def channel_mixer(x, shift, scale, norm_g, w_router, b_router, w_gate, w_up, w_down):
    B, T, D = x.shape
    h = (rms_norm(x, norm_g) * (1 + scale) + shift).reshape(B * T, D)
    comb = route(h, w_router, b_router).astype(h.dtype)
    y = jnp.zeros_like(h)
    for e in range(N_EXPERTS):
        a = jax.nn.silu(h @ w_gate[e]) * (h @ w_up[e])
        y = y + comb[:, e:e + 1] * (a @ w_down[e])
    return y.reshape(B, T, D)


def setup_inputs(seed: int = 0) -> dict:
    key = jax.random.key(seed)
    ks = jax.random.split(key, 24)
    n_pages = PAST_LEN // PAGE_SIZE
    n_used = DEC_BATCH * n_pages
    n_phys = n_used + max(1, n_used // 4)

    def nrm(k, shape, s):
        return jax.random.normal(k, shape, jnp.float32) * s

    page_table = jax.random.permutation(ks[8], n_phys)[:n_used].reshape(DEC_BATCH, n_pages).astype(jnp.int32)
    return {
        'x_prompt': nrm(ks[0], (BATCH, SEQ, D_MODEL), 1.0),
        'x_sample': nrm(ks[1], (DEC_BATCH, DEC_SEQ, D_MODEL), 1.0),
        'cache_k': nrm(ks[4], (DEPTH, n_phys, PAGE_SIZE, N_FOX_HEADS, HEAD_DIM), 1.0),
        'cache_v': nrm(ks[5], (DEPTH, n_phys, PAGE_SIZE, N_FOX_HEADS, HEAD_DIM), 1.0),
        'cache_logf': jax.nn.log_sigmoid(FORGET_BIAS_INIT + nrm(ks[6], (DEPTH, n_phys, PAGE_SIZE, N_FOX_HEADS), 1.0)),
        'state_ret': nrm(ks[7], (DEPTH, DEC_BATCH, N_RET_HEADS, HEAD_DIM, HEAD_DIM), 1.0),
        'page_table': page_table,
        'c_prompt': nrm(ks[2], (BATCH, D_MODEL), 1.0),
        'c_sample': nrm(ks[3], (DEC_BATCH, D_MODEL), 1.0),
        'w_ada': nrm(ks[9], (DEPTH, D_MODEL, N_MOD * D_MODEL), 0.5 * D_MODEL ** -0.5),
        'b_ada': nrm(ks[10], (DEPTH, N_MOD * D_MODEL), 0.02),
        'norm_mix_g': 1.0 + nrm(ks[11], (DEPTH, D_MODEL), 0.02),
        'norm_ffn_g': 1.0 + nrm(ks[12], (DEPTH, D_MODEL), 0.02),
        'w_in': nrm(ks[13], (DEPTH, D_MODEL, IN_WIDTH), D_MODEL ** -0.5),
        'b_forget': FORGET_BIAS_INIT + nrm(ks[14], (DEPTH, N_FOX_HEADS), 0.1),
        'q_norm_g': 1.0 + nrm(ks[15], (DEPTH, HEAD_DIM), 0.02),
        'k_norm_g': 1.0 + nrm(ks[16], (DEPTH, HEAD_DIM), 0.02),
        'ret_norm_g': 1.0 + nrm(ks[17], (DEPTH, RET_WIDTH), 0.02),
        'w_out': nrm(ks[18], (DEPTH, MIX_WIDTH, D_MODEL), MIX_WIDTH ** -0.5),
        'w_router': nrm(ks[19], (D_MODEL, N_EXPERTS), D_MODEL ** -0.5),
        'b_router': nrm(ks[20], (N_EXPERTS,), 0.01),
        'w_gate': nrm(ks[21], (DEPTH, N_EXPERTS, D_MODEL, D_EXPERT), D_MODEL ** -0.5),
        'w_up': nrm(ks[22], (DEPTH, N_EXPERTS, D_MODEL, D_EXPERT), D_MODEL ** -0.5),
        'w_down': nrm(ks[23], (DEPTH, N_EXPERTS, D_EXPERT, D_MODEL), D_EXPERT ** -0.5),
    }


def reference(x_prompt, x_sample, cache_k, cache_v, cache_logf, state_ret, page_table,
              c_prompt, c_sample, w_ada, b_ada, norm_mix_g, norm_ffn_g, w_in, b_forget,
              q_norm_g, k_norm_g, ret_norm_g, w_out, w_router, b_router, w_gate, w_up, w_down):
    S = x_prompt.shape[1]
    Bd, T = x_sample.shape[:2]
    past_len = page_table.shape[1] * PAGE_SIZE
    pos_p = jnp.arange(S)
    pos_s = past_len + jnp.arange(T)
    xp, xs = x_prompt, x_sample
    kp_l, vp_l, fp_l, sp_l = [], [], [], []
    ks_l, vs_l, fs_l, ss_l = [], [], [], []
    for l in range(DEPTH):
        sa, sc, ga, sm, scm, gm = adaln(c_prompt, w_ada[l], b_ada[l])
        rq, rk, rv, rg, fq, fk, fv, logf = mixer_in(xp, sa, sc, norm_mix_g[l], w_in[l], b_forget[l],
                                                    q_norm_g[l], k_norm_g[l], pos_p)
        ret_o, s_fin = retention_prompt(rq, rk, rv)
        fox_o = fox_prompt(fq, fk, fv, logf)
        xp = xp + ga * mixer_out(ret_o, rg, ret_norm_g[l], fox_o, w_out[l])
        xp = xp + gm * channel_mixer(xp, sm, scm, norm_ffn_g[l], w_router, b_router,
                                     w_gate[l], w_up[l], w_down[l])
        kp_l.append(fk)
        vp_l.append(fv)
        fp_l.append(logf)
        sp_l.append(s_fin)
        sa, sc, ga, sm, scm, gm = adaln(c_sample, w_ada[l], b_ada[l])
        rq, rk, rv, rg, fq, fk, fv, logf = mixer_in(xs, sa, sc, norm_mix_g[l], w_in[l], b_forget[l],
                                                    q_norm_g[l], k_norm_g[l], pos_s)
        s_new, ret_o = retention_chunk(state_ret[l], rq.transpose(0, 2, 1, 3),
                                       rk.transpose(0, 2, 1, 3), rv.transpose(0, 2, 1, 3))
        ret_o = ret_o.transpose(0, 2, 1, 3)
        k_past = cache_k[l, page_table].reshape(Bd, past_len, N_FOX_HEADS, HEAD_DIM)
        v_past = cache_v[l, page_table].reshape(Bd, past_len, N_FOX_HEADS, HEAD_DIM)
        logf_past = cache_logf[l, page_table].reshape(Bd, past_len, N_FOX_HEADS)
        fox_o = fox_sample(fq, fk, fv, logf, k_past, v_past, logf_past)
        xs = xs + ga * mixer_out(ret_o, rg, ret_norm_g[l], fox_o, w_out[l])
        xs = xs + gm * channel_mixer(xs, sm, scm, norm_ffn_g[l], w_router, b_router,
                                     w_gate[l], w_up[l], w_down[l])
        ks_l.append(fk)
        vs_l.append(fv)
        fs_l.append(logf)
        ss_l.append(s_new)
    return (xp, xs, jnp.stack(kp_l), jnp.stack(vp_l), jnp.stack(fp_l), jnp.stack(sp_l),
            jnp.stack(ks_l), jnp.stack(vs_l), jnp.stack(fs_l), jnp.stack(ss_l))
```

```python
import functools

import jax
import jax.numpy as jnp
from jax import lax
from jax.experimental import pallas as pl
from jax.experimental.pallas import tpu as pltpu

f32, bf16 = jnp.float32, jnp.bfloat16

HEAD_DIM = 128
N_HEADS = 8
GROUP_WIDTH = N_HEADS * HEAD_DIM
RET_CHUNK = 128
PAGE_SIZE = 128
ROPE_BASE = 10000.0
N_EXPERTS = 16
EXPERTS_PER_GROUP = 4
EPS = 1e-6
LANES = 128
V7X_VMEM_LIMIT = 56 * 1024 * 1024
NEG_BIG = -1e30

_NT = (((1,), (1,)), ((), ()))
_TN = (((0,), (0,)), ((), ()))


def _params(n_axes):
    return pltpu.CompilerParams(dimension_semantics=("arbitrary",) * n_axes, vmem_limit_bytes=V7X_VMEM_LIMIT)


def _silu(x):
    return x * jax.nn.sigmoid(x)


def _log_sigmoid(x):
    return jnp.minimum(x, 0.0) - jnp.log1p(jnp.exp(-jnp.abs(x)))


def _adaln_body(c_ref, w_ref, b_ref, o_ref):
    s = _silu(c_ref[...]).astype(bf16)
    o_ref[...] = jnp.dot(s, w_ref[...].astype(bf16), preferred_element_type=f32) + b_ref[...]


def _adaln(c_all, w_ada, b_ada):
    depth, d, m6 = w_ada.shape
    nmod, r, tn = m6 // d, c_all.shape[0], 512
    nj = d // tn
    return pl.pallas_call(
        _adaln_body, grid=(depth, nmod, nj),
        in_specs=[pl.BlockSpec((r, d), lambda l, c, j: (0, 0)),
                  pl.BlockSpec((None, d, tn), lambda l, c, j: (l, 0, c * nj + j)),
                  pl.BlockSpec((None, 1, tn), lambda l, c, j: (l, 0, c * nj + j))],
        out_specs=pl.BlockSpec((None, None, r, tn), lambda l, c, j: (l, c, 0, j)),
        out_shape=jax.ShapeDtypeStruct((depth, nmod, r, d), f32),
        compiler_params=_params(3), name="adaln")(c_all, w_ada, b_ada.reshape(depth, 1, m6))


class _Mod:
    def __init__(self, mod, layer, n_sample, rows_per_seq):
        self.mod, self.layer, self.n_sample, self.rows_per_seq = mod, layer, n_sample, rows_per_seq

    def operand(self, chunk, tm, tn, row_of, col_of):
        depth, nmod, r, d = self.mod.shape
        l, bd, rps = self.layer, self.n_sample, self.rows_per_seq
        if rps is None:
            assert tm == bd
            return self.mod, pl.BlockSpec((None, None, tm, tn), lambda *g: (l, chunk, 0, col_of(*g)))
        return (self.mod.reshape(depth, nmod, r, 1, d),
                pl.BlockSpec((None, None, None, 1, tn), lambda *g: (l, chunk, bd + (row_of(*g) * tm) // rps, 0, col_of(*g))))


def _modulated(x_ref, g_ref, sc_ref, sh_ref):
    x = x_ref[...]
    y = x * lax.rsqrt(jnp.mean(x * x, axis=-1, keepdims=True) + EPS) * g_ref[...]
    return y * (1.0 + sc_ref[...]) + sh_ref[...]


def _modulate_body(x_ref, g_ref, sc_ref, sh_ref, o_ref):
    o_ref[...] = _modulated(x_ref, g_ref, sc_ref, sh_ref).astype(o_ref.dtype)


def _route(logits, b_router):
    scores = jax.nn.sigmoid(logits)
    biased = scores + b_router
    lane_i = lax.broadcasted_iota(jnp.int32, logits.shape, 1)
    lane = lane_i.astype(f32)
    grp = lane_i >> 2
    neg = -jnp.inf

    def top2(v):
        t1 = jnp.max(v, axis=-1, keepdims=True)
        i1 = jnp.min(jnp.where(v == t1, lane, float(LANES)), axis=-1, keepdims=True)
        v2 = jnp.where(lane == i1, neg, v)
        t2 = jnp.max(v2, axis=-1, keepdims=True)
        i2 = jnp.min(jnp.where(v2 == t2, lane, float(LANES)), axis=-1, keepdims=True)
        return t1, i1, t2, i2

    best, best_g = None, None
    for g in range(N_EXPERTS // EXPERTS_PER_GROUP):
        t1, _, t2, _ = top2(jnp.where(grp == g, biased, neg))
        gs = t1 + t2
        if best is None:
            best, best_g = gs, jnp.zeros(gs.shape, jnp.int32)
        else:
            take = gs > best
            best_g = jnp.where(take, g, best_g)
            best = jnp.where(take, gs, best)
    _, i1, _, i2 = top2(jnp.where(grp == best_g, biased, neg))
    s1 = jnp.sum(jnp.where(lane == i1, scores, 0.0), axis=-1, keepdims=True)
    s2 = jnp.sum(jnp.where(lane == i2, scores, 0.0), axis=-1, keepdims=True)
    den = s1 + s2
    return jnp.where(lane == i1, s1 / den, 0.0) + jnp.where(lane == i2, s2 / den, 0.0)


def _modulate_route_body(x_ref, g_ref, sc_ref, sh_ref, wr_ref, br_ref, o_ref, comb_ref):
    h = _modulated(x_ref, g_ref, sc_ref, sh_ref)
    o_ref[...] = h.astype(o_ref.dtype)
    logits = jnp.dot(h, wr_ref[...], precision=lax.Precision.HIGHEST, preferred_element_type=f32)
    comb_ref[...] = _route(logits, br_ref[...])


def _modulate(x, norm_g, mod, sc_chunk, sh_chunk, tm, router=None):
    n, d = x.shape
    row, col = (lambda i: i), (lambda i: 0)
    sc, sc_spec = mod.operand(sc_chunk, tm, d, row, col)
    sh, sh_spec = mod.operand(sh_chunk, tm, d, row, col)
    in_specs = [pl.BlockSpec((tm, d), lambda i: (i, 0)), pl.BlockSpec((1, d), lambda i: (0, 0)), sc_spec, sh_spec]
    args = [x, norm_g.reshape(1, d), sc, sh]
    out_specs = pl.BlockSpec((tm, d), lambda i: (i, 0))
    out_shape = jax.ShapeDtypeStruct((n, d), bf16)
    body = _modulate_body
    if router is not None:
        w_router_pad, b_router_pad = router
        in_specs += [pl.BlockSpec((d, LANES), lambda i: (0, 0)), pl.BlockSpec((1, LANES), lambda i: (0, 0))]
        args += [w_router_pad, b_router_pad]
        out_specs = [out_specs, pl.BlockSpec((tm, LANES), lambda i: (i, 0))]
        out_shape = [out_shape, jax.ShapeDtypeStruct((n, LANES), f32)]
        body = _modulate_route_body
    return pl.pallas_call(body, grid=(n // tm,), in_specs=in_specs, out_specs=out_specs, out_shape=out_shape,
                          compiler_params=_params(1), name="modulate")(*args)


def _proj_body(epilogue, n_extra, h_ref, w_ref, *rest):
    extra, outs, wb_ref = rest[:n_extra], rest[n_extra:-1], rest[-1]

    @pl.when(pl.program_id(1) == 0)
    def _():
        wb_ref[...] = w_ref[...].astype(bf16)

    acc = jnp.dot(h_ref[...], wb_ref[...], preferred_element_type=f32)
    epilogue(acc, extra, outs)


def _heads(width):
    return [slice(h * HEAD_DIM, (h + 1) * HEAD_DIM) for h in range(width // HEAD_DIM)]


def _epi_rope(acc, extra, outs):
    cos, sin = extra[0][...], extra[1][...]
    kscale = jnp.where(pl.program_id(0) == 1, HEAD_DIM ** -0.5, 1.0).astype(f32)
    for hs in _heads(acc.shape[1]):
        a = acc[:, hs]
        outs[0][:, hs] = ((a * cos + pltpu.roll(a, HEAD_DIM // 2, 1) * sin) * kscale).astype(outs[0].dtype)


def _epi_headnorm(acc, extra, outs):
    g = extra[0][...]
    for hs in _heads(acc.shape[1]):
        a = acc[:, hs]
        y = a * lax.rsqrt(jnp.mean(a * a, axis=-1, keepdims=True) + EPS) * g
        for o in outs:
            o[:, hs] = y.astype(o.dtype)


def _epi_plain(acc, extra, outs):
    for o in outs:
        o[...] = acc.astype(o.dtype)


def _epi_forget(acc, extra, outs):
    y = _log_sigmoid(acc + extra[0][...])
    outs[0][...] = y[:, :outs[0].shape[1]]


def _proj(h16, w, layer, col0, ncol, tn, tm, epilogue, extra, out_dtypes, out_width=None):
    n, k = h16.shape
    out_width = tn if out_width is None else out_width
    in_specs = [pl.BlockSpec((tm, k), lambda j, i: (i, 0)),
                pl.BlockSpec((None, k, tn), lambda j, i: (layer, 0, col0 + j))] + [s for _, s in extra]
    out_specs = [pl.BlockSpec((tm, out_width), lambda j, i: (i, j)) for _ in out_dtypes]
    out_shape = [jax.ShapeDtypeStruct((n, ncol * out_width), dt) for dt in out_dtypes]
    return pl.pallas_call(
        functools.partial(_proj_body, epilogue, len(extra)), grid=(ncol, n // tm),
        in_specs=in_specs, out_specs=out_specs, out_shape=out_shape,
        scratch_shapes=[pltpu.VMEM((k, tn), bf16)],
        compiler_params=_params(2), name="proj_" + epilogue.__name__[5:])(h16, w, *[a for a, _ in extra])


def _group_norm_gate(o, gain, gate):
    mu = jnp.mean(o, axis=-1, keepdims=True)
    var = jnp.mean(jnp.square(o - mu), axis=-1, keepdims=True)
    return (o - mu) * lax.rsqrt(var + EPS) * gain * _silu(gate)


def _ret_prompt_body(q_ref, k_ref, v_ref, g_ref, intra_ref, crossd_ref, kdec_ref, sdec_ref, gn_ref, o_ref, st_ref):
    @pl.when(pl.program_id(1) == 0)
    def _():
        st_ref[...] = jnp.zeros(st_ref.shape, f32)

    for h, hs in enumerate(_heads(q_ref.shape[1])):
        q, k, v = q_ref[:, hs], k_ref[:, hs], v_ref[:, hs]
        state = st_ref[h]
        scores = lax.dot_general(q, k, _NT, preferred_element_type=f32) * intra_ref[h]
        o = jnp.dot(scores.astype(bf16), v, preferred_element_type=f32)
        o = o + jnp.dot(q, state.astype(bf16), preferred_element_type=f32) * crossd_ref[h]
        k_dec = (k.astype(f32) * kdec_ref[h]).astype(bf16)
        st_ref[h] = state * sdec_ref[h] + lax.dot_general(k_dec, v, _TN, preferred_element_type=f32)
        o_ref[:, hs] = _group_norm_gate(o, gn_ref[:, hs], g_ref[:, hs].astype(f32)).astype(o_ref.dtype)


def _ret_tables():
    lg = jnp.log1p(-jnp.exp2(-5.0 - jnp.arange(N_HEADS, dtype=f32)))[:, None, None]
    t = jnp.arange(RET_CHUNK, dtype=f32)
    diff = t[None, :, None] - t[None, None, :]
    intra = jnp.where(diff >= 0, jnp.exp(lg * jnp.maximum(diff, 0.0)), 0.0)
    ones = jnp.ones((1, 1, HEAD_DIM), f32)
    crossd = jnp.exp(lg * (t[None, :, None] + 1.0)) * ones
    kdec = jnp.exp(lg * (RET_CHUNK - 1.0 - t[None, :, None])) * ones
    sdec = jnp.exp(lg * RET_CHUNK) * ones
    step_dec = jnp.exp(lg) * ones
    return intra, crossd, kdec, sdec, step_dec


def _ret_prompt(qk16, vg16, ret_norm_g, batch, tables):
    n = qk16.shape[0]
    nc = n // batch // RET_CHUNK
    intra, crossd, kdec, sdec, _ = tables
    row = lambda b, c: b * nc + c
    full3 = lambda a: pl.BlockSpec(a.shape, lambda b, c: (0, 0, 0))
    blk = lambda col: pl.BlockSpec((RET_CHUNK, GROUP_WIDTH), lambda b, c: (row(b, c), col))
    return pl.pallas_call(
        _ret_prompt_body, grid=(batch, nc),
        in_specs=[blk(0), blk(1), blk(0), blk(1), full3(intra), full3(crossd), full3(kdec), full3(sdec),
                  pl.BlockSpec((1, GROUP_WIDTH), lambda b, c: (0, 0))],
        out_specs=[blk(0), pl.BlockSpec((None, N_HEADS, HEAD_DIM, HEAD_DIM), lambda b, c: (b, 0, 0, 0))],
        out_shape=[jax.ShapeDtypeStruct((n, GROUP_WIDTH), bf16),
                   jax.ShapeDtypeStruct((batch, N_HEADS, HEAD_DIM, HEAD_DIM), f32)],
        compiler_params=_params(2), name="ret_prompt")(
            qk16, qk16, vg16, vg16, intra, crossd, kdec, sdec, ret_norm_g.reshape(1, GROUP_WIDTH))


def _block_diag(x):
    xt = jnp.tile(x, (1, N_HEADS))
    lane = lax.broadcasted_iota(jnp.int32, xt.shape, 1)
    row = lax.broadcasted_iota(jnp.int32, xt.shape, 0)
    return jnp.where((lane >> 7) == row, xt, 0.0)


def _ret_sample_body(q_ref, k_ref, v_ref, g_ref, st_ref, dec3_ref, dec2_ref, gn_ref, o_ref, sn_ref):
    q16, k16, v16 = q_ref[...].astype(bf16), k_ref[...].astype(bf16), v_ref[...].astype(bf16)
    q, k, v = q16.astype(f32), k16.astype(f32), v16.astype(f32)
    state = st_ref[...]
    cross = jnp.dot(_block_diag(q).astype(bf16), state.reshape(GROUP_WIDTH, HEAD_DIM).astype(bf16),
                    preferred_element_type=f32) * dec2_ref[...]
    qk = jnp.sum(q * k, axis=-1, keepdims=True).astype(bf16).astype(f32)
    o = qk * v + cross
    upd = lax.dot_general(_block_diag(k).astype(bf16), v16, _TN, preferred_element_type=f32)
    sn_ref[...] = state * dec3_ref[...] + upd.reshape(state.shape)
    o_ref[...] = _group_norm_gate(o, gn_ref[...], g_ref[...]).astype(o_ref.dtype)


def _ret_sample(q3, k3, v3, g3, state_ret, layer, ret_norm_g, step_dec):
    bd = q3.shape[0]
    tok = pl.BlockSpec((None, N_HEADS, HEAD_DIM), lambda b: (b, 0, 0))
    return pl.pallas_call(
        _ret_sample_body, grid=(bd,),
        in_specs=[tok, tok, tok, tok,
                  pl.BlockSpec((None, None, N_HEADS, HEAD_DIM, HEAD_DIM), lambda b: (layer, b, 0, 0, 0)),
                  pl.BlockSpec(step_dec.shape, lambda b: (0, 0, 0)),
                  pl.BlockSpec((N_HEADS, HEAD_DIM), lambda b: (0, 0)),
                  pl.BlockSpec((N_HEADS, HEAD_DIM), lambda b: (0, 0))],
        out_specs=[tok, pl.BlockSpec((None, N_HEADS, HEAD_DIM, HEAD_DIM), lambda b: (b, 0, 0, 0))],
        out_shape=[jax.ShapeDtypeStruct((bd, N_HEADS, HEAD_DIM), f32),
                   jax.ShapeDtypeStruct((bd, N_HEADS, HEAD_DIM, HEAD_DIM), f32)],
        compiler_params=_params(1), name="ret_sample")(
            q3, k3, v3, g3, state_ret, step_dec, step_dec.reshape(N_HEADS, HEAD_DIM),
            ret_norm_g.reshape(N_HEADS, HEAD_DIM))


def _cumsum_body(x_ref, o_ref, carry_ref):
    @pl.when(pl.program_id(1) == 0)
    def _():
        carry_ref[...] = jnp.zeros(carry_ref.shape, f32)

    tb = x_ref.shape[1]
    upper = (lax.broadcasted_iota(jnp.int32, (tb, tb), 0) <= lax.broadcasted_iota(jnp.int32, (tb, tb), 1)).astype(f32)
    cum = jnp.dot(x_ref[...], upper, precision=lax.Precision.HIGHEST, preferred_element_type=f32) + carry_ref[:, :1]
    o_ref[...] = cum
    carry_ref[...] = jnp.broadcast_to(cum[:, tb - 1:tb], carry_ref.shape)


def _cumsum_lanes(x_t, tb):
    b, h, s = x_t.shape
    spec = pl.BlockSpec((None, h, tb), lambda i, j: (i, 0, j))
    return pl.pallas_call(_cumsum_body, grid=(b, s // tb), in_specs=[spec], out_specs=spec,
                          out_shape=jax.ShapeDtypeStruct(x_t.shape, f32),
                          scratch_shapes=[pltpu.VMEM((h, LANES), f32)],
                          compiler_params=_params(2), name="forget_cumsum")(x_t)


def _fox_prompt_body(q_ref, k_ref, v_ref, fcol_ref, frow_ref, o_ref, m_ref, l_ref, acc_ref):
    head, qi = pl.program_id(1), pl.program_id(2)
    tq = q_ref.shape[0]
    scale = HEAD_DIM ** -0.5
    q = q_ref[...]
    lane = lax.broadcasted_iota(jnp.int32, fcol_ref.shape, 1)
    fq = jnp.sum(jnp.where(lane == head, fcol_ref[...], 0.0), axis=-1, keepdims=True)
    m_ref[...] = jnp.full(m_ref.shape, NEG_BIG, f32)
    l_ref[...] = jnp.zeros(l_ref.shape, f32)
    acc_ref[...] = jnp.zeros(acc_ref.shape, f32)

    def step(j, diagonal):
        off = pl.multiple_of(j * tq, tq)
        ks, vs = k_ref[pl.ds(off, tq), :], v_ref[pl.ds(off, tq), :]
        s = lax.dot_general(q, ks, _NT, preferred_element_type=f32) * scale + (fq - frow_ref[j])
        if diagonal:
            r = lax.broadcasted_iota(jnp.int32, s.shape, 0)
            c = lax.broadcasted_iota(jnp.int32, s.shape, 1)
            s = jnp.where(c <= r, s, -jnp.inf)
        m_old = m_ref[...]
        m_new = jnp.maximum(m_old, jnp.max(s, axis=-1, keepdims=True))
        alpha = jnp.exp(m_old - m_new)
        p = jnp.exp(s - m_new)
        l_ref[...] = alpha * l_ref[...] + jnp.sum(p, axis=-1, keepdims=True)
        acc_ref[...] = alpha * acc_ref[...] + jnp.dot(p.astype(bf16), vs, preferred_element_type=f32)
        m_ref[...] = m_new

    def off_diagonal(j, carry):
        step(j, False)
        return carry

    lax.fori_loop(0, qi, off_diagonal, 0)
    step(qi, True)
    o_ref[...] = (acc_ref[...] / l_ref[...]).astype(o_ref.dtype)


def _fox_prompt(q16, k16, v16, f_cols, f_rows, batch, tq):
    n = q16.shape[0]
    s = n // batch
    nq = s // tq
    kv = lambda a: a.reshape(batch, s, GROUP_WIDTH)
    kv_spec = pl.BlockSpec((None, s, HEAD_DIM), lambda b, h, i: (b, 0, h))
    q_spec = pl.BlockSpec((tq, HEAD_DIM), lambda b, h, i: (b * nq + i, h))
    return pl.pallas_call(
        _fox_prompt_body, grid=(batch, N_HEADS, nq),
        in_specs=[q_spec, kv_spec, kv_spec,
                  pl.BlockSpec((None, tq, N_HEADS), lambda b, h, i: (b, i, 0)),
                  pl.BlockSpec((None, None, nq, 1, tq), lambda b, h, i: (b, h, 0, 0, 0))],
        out_specs=q_spec,
        out_shape=jax.ShapeDtypeStruct((n, GROUP_WIDTH), bf16),
        scratch_shapes=[pltpu.VMEM((tq, 1), f32), pltpu.VMEM((tq, 1), f32), pltpu.VMEM((tq, HEAD_DIM), f32)],
        compiler_params=_params(3), name="fox_prompt")(
            q16, kv(k16), kv(v16), f_cols, f_rows.reshape(batch, N_HEADS, nq, 1, tq))


def _suffix_body(pt_ref, x_ref, new_ref, o_ref, carry_ref):
    @pl.when(pl.program_id(1) == 0)
    def _():
        carry_ref[...] = new_ref[...]

    x = x_ref[...]
    t = x.shape[0]
    later = (lax.broadcasted_iota(jnp.int32, (t, t), 1) > lax.broadcasted_iota(jnp.int32, (t, t), 0)).astype(f32)
    o_ref[...] = jnp.dot(later, x, precision=lax.Precision.HIGHEST, preferred_element_type=f32) + carry_ref[...]
    carry_ref[...] = carry_ref[...] + jnp.sum(x, axis=0, keepdims=True)


def _forget_suffix(cache_logf, layer, page_table, logf_new):
    bd, n_pages = page_table.shape
    h = cache_logf.shape[-1]
    page = lambda b, pp, pt: n_pages - 1 - pp
    grid_spec = pltpu.PrefetchScalarGridSpec(
        num_scalar_prefetch=1, grid=(bd, n_pages),
        in_specs=[pl.BlockSpec((None, None, PAGE_SIZE, h), lambda b, pp, pt: (layer, pt[b, page(b, pp, pt)], 0, 0)),
                  pl.BlockSpec((None, 1, h), lambda b, pp, pt: (b, 0, 0))],
        out_specs=pl.BlockSpec((None, None, PAGE_SIZE, h), lambda b, pp, pt: (b, page(b, pp, pt), 0, 0)),
        scratch_shapes=[pltpu.VMEM((1, h), f32)])
    return pl.pallas_call(_suffix_body, grid_spec=grid_spec,
                          out_shape=jax.ShapeDtypeStruct((bd, n_pages, PAGE_SIZE, h), f32),
                          compiler_params=_params(2), name="forget_suffix")(
                              page_table, cache_logf, logf_new.reshape(bd, 1, h))


def _fox_sample_body(pt_ref, q_ref, kn_ref, vn_ref, k_ref, v_ref, bias_ref, o_ref, qbd_ref, m_ref, l_ref, acc_ref):
    p = pl.program_id(1)
    scale = HEAD_DIM ** -0.5

    @pl.when(p == 0)
    def _():
        qbd_ref[...] = _block_diag(q_ref[...]).astype(bf16)
        m_ref[...] = jnp.full(m_ref.shape, NEG_BIG, f32)
        l_ref[...] = jnp.zeros(l_ref.shape, f32)
        acc_ref[...] = jnp.zeros(acc_ref.shape, f32)

    s = lax.dot_general(qbd_ref[...], k_ref[...].astype(bf16), _NT, preferred_element_type=f32) * scale + bias_ref[...]
    m_old = m_ref[...]
    m_new = jnp.maximum(m_old, jnp.max(s, axis=-1, keepdims=True))
    alpha = jnp.exp(m_old - m_new)
    pe = jnp.exp(s - m_new)
    l_ref[...] = alpha * l_ref[...] + jnp.sum(pe, axis=-1, keepdims=True)
    acc_ref[...] = alpha * acc_ref[...] + jnp.dot(pe.astype(bf16), v_ref[...].astype(bf16), preferred_element_type=f32)
    m_ref[...] = m_new

    @pl.when(p == pl.num_programs(1) - 1)
    def _():
        rounded = lambda ref: ref[...].astype(bf16).astype(f32)
        s_new = jnp.sum(rounded(q_ref) * rounded(kn_ref), axis=-1, keepdims=True) * scale
        m_old = m_ref[...]
        m_fin = jnp.maximum(m_old, s_new)
        alpha = jnp.exp(m_old - m_fin)
        p_new = jnp.exp(s_new - m_fin)
        acc = acc_ref[...]
        row = lax.broadcasted_iota(jnp.int32, (N_HEADS, HEAD_DIM), 0)
        diag = jnp.zeros((N_HEADS, HEAD_DIM), f32)
        for h, hs in enumerate(_heads(GROUP_WIDTH)):
            diag = diag + jnp.where(row == h, acc[:, hs], 0.0)
        num = alpha * diag + p_new.astype(bf16).astype(f32) * rounded(vn_ref)
        o_ref[...] = (num / (alpha * l_ref[...] + p_new)).astype(o_ref.dtype)


def _fox_sample(q3, kn3, vn3, cache_k, cache_v, layer, page_table, bias_t):
    bd, n_pages = page_table.shape
    depth, n_phys = cache_k.shape[:2]
    ck = cache_k.reshape(depth, n_phys, PAGE_SIZE, GROUP_WIDTH)
    cv = cache_v.reshape(depth, n_phys, PAGE_SIZE, GROUP_WIDTH)
    tok = pl.BlockSpec((None, N_HEADS, HEAD_DIM), lambda b, p, pt: (b, 0, 0))
    page = pl.BlockSpec((None, None, PAGE_SIZE, GROUP_WIDTH), lambda b, p, pt: (layer, pt[b, p], 0, 0))
    grid_spec = pltpu.PrefetchScalarGridSpec(
        num_scalar_prefetch=1, grid=(bd, n_pages),
        in_specs=[tok, tok, tok, page, page, pl.BlockSpec((None, N_HEADS, PAGE_SIZE), lambda b, p, pt: (b, 0, p))],
        out_specs=tok,
        scratch_shapes=[pltpu.VMEM((N_HEADS, GROUP_WIDTH), bf16), pltpu.VMEM((N_HEADS, 1), f32),
                        pltpu.VMEM((N_HEADS, 1), f32), pltpu.VMEM((N_HEADS, GROUP_WIDTH), f32)])
    return pl.pallas_call(_fox_sample_body, grid_spec=grid_spec,
                          out_shape=jax.ShapeDtypeStruct((bd, N_HEADS, HEAD_DIM), f32),
                          compiler_params=_params(2), name="fox_sample")(page_table, q3, kn3, vn3, ck, cv, bias_t)


def _out_proj_body(ret_ref, fox_ref, w_ref, x_ref, ga_ref, o_ref, wb_ref):
    @pl.when(pl.program_id(1) == 0)
    def _():
        wb_ref[...] = w_ref[...].astype(bf16)

    half = ret_ref.shape[1]
    y = jnp.dot(ret_ref[...].astype(bf16), wb_ref[:half, :], preferred_element_type=f32)
    y = y + jnp.dot(fox_ref[...].astype(bf16), wb_ref[half:, :], preferred_element_type=f32)
    o_ref[...] = x_ref[...] + ga_ref[...] * y


def _out_proj(ret16, fox16, w_out, layer, x, mod, tm, tn):
    n, d = x.shape
    k = w_out.shape[1]
    ga, ga_spec = mod.operand(2, tm, tn, lambda j, i: i, lambda j, i: j)
    act = pl.BlockSpec((tm, ret16.shape[1]), lambda j, i: (i, 0))
    xs = pl.BlockSpec((tm, tn), lambda j, i: (i, j))
    return pl.pallas_call(
        _out_proj_body, grid=(d // tn, n // tm),
        in_specs=[act, act, pl.BlockSpec((None, k, tn), lambda j, i: (layer, 0, j)), xs, ga_spec],
        out_specs=xs, out_shape=jax.ShapeDtypeStruct((n, d), f32),
        scratch_shapes=[pltpu.VMEM((k, tn), bf16)],
        compiler_params=_params(2), name="out_proj")(ret16, fox16, w_out, x, ga)


def _moe_body(h_ref, comb_ref, wg_ref, wu_ref, wd_ref, x_ref, gm_ref, o_ref, acc_ref):
    e = pl.program_id(1)

    @pl.when(e == 0)
    def _():
        acc_ref[...] = jnp.zeros(acc_ref.shape, f32)

    h = h_ref[...]
    g = jnp.dot(h, wg_ref[...], preferred_element_type=f32)
    u = jnp.dot(h, wu_ref[...], preferred_element_type=f32)
    y = jnp.dot((_silu(g) * u).astype(bf16), wd_ref[...], preferred_element_type=f32)
    lane = lax.broadcasted_iota(jnp.int32, comb_ref.shape, 1)
    w = jnp.sum(jnp.where(lane == e, comb_ref[...], 0.0), axis=-1, keepdims=True)
    acc_ref[...] += w * y

    @pl.when(e == pl.num_programs(1) - 1)
    def _():
        o_ref[...] = x_ref[...] + gm_ref[...] * acc_ref[...]


def _moe(h16, comb, wg16, wu16, wd16, layer, x, mod, tm):
    n, d = x.shape
    n_exp, _, de = wg16.shape[1:]
    gm, gm_spec = mod.operand(5, tm, d, lambda i, e: i, lambda i, e: 0)
    rows = lambda w: pl.BlockSpec((tm, w), lambda i, e: (i, 0))
    return pl.pallas_call(
        _moe_body, grid=(n // tm, n_exp),
        in_specs=[rows(d), rows(LANES),
                  pl.BlockSpec((None, None, d, de), lambda i, e: (layer, e, 0, 0)),
                  pl.BlockSpec((None, None, d, de), lambda i, e: (layer, e, 0, 0)),
                  pl.BlockSpec((None, None, de, d), lambda i, e: (layer, e, 0, 0)),
                  rows(d), gm_spec],
        out_specs=rows(d), out_shape=jax.ShapeDtypeStruct((n, d), f32),
        scratch_shapes=[pltpu.VMEM((tm, d), f32)],
        compiler_params=_params(2), name="moe")(h16, comb, wg16, wu16, wd16, x, gm)


def _rope_tables(positions):
    half = HEAD_DIM // 2
    inv_freq = ROPE_BASE ** (-jnp.arange(half, dtype=f32) / half)
    ang = positions.astype(f32)[:, None] * inv_freq[None, :]
    cos, sin = jnp.cos(ang), jnp.sin(ang)
    return jnp.concatenate([cos, cos], axis=-1), jnp.concatenate([-sin, sin], axis=-1)


def _mixer_in(x, mod, layer, tm, pos_tables, pos_tiles, p, lo):
    h16 = _modulate(x, p["norm_mix_g"][layer], mod, 1, 0, tm)
    cos, sin = pos_tables
    pos_spec = pl.BlockSpec((tm, HEAD_DIM), lambda j, i: (i % pos_tiles, 0))
    w_in = p["w_in"]
    gw = GROUP_WIDTH
    (qk16,) = _proj(h16, w_in, layer, 0, 2, gw, tm, _epi_rope, [(cos, pos_spec), (sin, pos_spec)], [lo])
    (vg16,) = _proj(h16, w_in, layer, 2, 2, gw, tm, _epi_plain, [], [lo])
    gain = lambda g: (g[layer].reshape(1, HEAD_DIM), pl.BlockSpec((1, HEAD_DIM), lambda j, i: (0, 0)))
    (fq16,) = _proj(h16, w_in, layer, 4, 1, gw, tm, _epi_headnorm, [gain(p["q_norm_g"])], [lo])
    fk32, fk16 = _proj(h16, w_in, layer, 5, 1, gw, tm, _epi_headnorm, [gain(p["k_norm_g"])], [f32, lo])
    fv32, fv16 = _proj(h16, w_in, layer, 6, 1, gw, tm, _epi_plain, [], [f32, lo])
    n_forget = p["b_forget"].shape[1]
    w_f = jnp.pad(w_in[layer, :, 7 * gw:], ((0, 0), (0, LANES - n_forget)))[None]
    b_f = jnp.pad(p["b_forget"][layer], (0, LANES - n_forget)).reshape(1, LANES)
    (logf,) = _proj(h16, w_f, 0, 0, 1, LANES, tm, _epi_forget,
                    [(b_f, pl.BlockSpec((1, LANES), lambda j, i: (0, 0)))], [f32], out_width=n_forget)
    return qk16, vg16, fq16, fk32, fk16, fv32, fv16, logf


def _channel(x, mod, layer, tm, tm_moe, p):
    h16, comb = _modulate(x, p["norm_ffn_g"][layer], mod, 4, 3, tm, router=(p["w_router_pad"], p["b_router_pad"]))
    return _moe(h16, comb, p["wg16"], p["wu16"], p["wd16"], layer, x, mod, tm_moe)


def kernel(x_prompt, x_sample, cache_k, cache_v, cache_logf, state_ret, page_table, c_prompt, c_sample, w_ada, b_ada,
           norm_mix_g, norm_ffn_g, w_in, b_forget, q_norm_g, k_norm_g, ret_norm_g, w_out, w_router, b_router,
           w_gate, w_up, w_down):
    batch, seq, d = x_prompt.shape
    bd, t_new, _ = x_sample.shape
    assert t_new == 1, "one new token per sampled sequence"
    assert bd % 8 == 0 and d == 2 * GROUP_WIDTH
    depth = w_in.shape[0]
    n_pages = page_table.shape[1]
    past_len = n_pages * PAGE_SIZE
    n_p = batch * seq
    tm_p = min(512, seq)
    tq = min(512, seq)

    p = dict(norm_mix_g=norm_mix_g, norm_ffn_g=norm_ffn_g, w_in=w_in, b_forget=b_forget, q_norm_g=q_norm_g,
             k_norm_g=k_norm_g,
             w_router_pad=jnp.pad(w_router, ((0, 0), (0, LANES - w_router.shape[1]))),
             b_router_pad=jnp.pad(b_router, (0, LANES - b_router.shape[0])).reshape(1, LANES),
             wg16=w_gate.astype(bf16), wu16=w_up.astype(bf16), wd16=w_down.astype(bf16))

    r_rows = -(-(bd + batch) // 8) * 8
    c_all = jnp.concatenate([c_sample, c_prompt, jnp.zeros((r_rows - bd - batch, d), f32)], axis=0)
    mod_all = _adaln(c_all, w_ada, b_ada)

    tables = _ret_tables()
    rope_p = _rope_tables(jnp.arange(seq))
    rope_s = _rope_tables(jnp.full((bd,), past_len))

    xp = x_prompt.reshape(n_p, d)
    xs = x_sample.reshape(bd, d)
    outs = [[] for _ in range(8)]
    tok3 = lambda a: a.reshape(bd, N_HEADS, HEAD_DIM)
    for l in range(depth):
        mod = _Mod(mod_all, l, bd, seq)
        qk16, vg16, fq16, fk32, fk16, fv32, fv16, logf = _mixer_in(xp, mod, l, tm_p, rope_p, seq // tm_p, p, bf16)
        ret16, s_fin = _ret_prompt(qk16, vg16, ret_norm_g[l], batch, tables)
        logf_b = logf.reshape(batch, seq, -1)
        f_rows = _cumsum_lanes(logf_b.transpose(0, 2, 1), tq)
        fox16 = _fox_prompt(fq16, fk16, fv16, f_rows.transpose(0, 2, 1), f_rows, batch, tq)
        xp = _out_proj(ret16, fox16, w_out, l, xp, mod, tm_p, GROUP_WIDTH)
        xp = _channel(xp, mod, l, tm_p, tm_p, p)
        outs[0].append(fk32.reshape(batch, seq, N_HEADS, HEAD_DIM))
        outs[1].append(fv32.reshape(batch, seq, N_HEADS, HEAD_DIM))
        outs[2].append(logf_b)
        outs[3].append(s_fin)
        mod = _Mod(mod_all, l, bd, None)
        qk16, vg16, fq16, fk32, fk16, fv32, fv16, logf = _mixer_in(xs, mod, l, bd, rope_s, 1, p, f32)
        ret3, s_new = _ret_sample(tok3(qk16[:, :GROUP_WIDTH]), tok3(qk16[:, GROUP_WIDTH:]), tok3(vg16[:, :GROUP_WIDTH]),
                                  tok3(vg16[:, GROUP_WIDTH:]), state_ret, l, ret_norm_g[l], tables[4])
        bias = _forget_suffix(cache_logf, l, page_table, logf)
        bias_t = bias.reshape(bd, past_len, -1).transpose(0, 2, 1)
        fox3 = _fox_sample(tok3(fq16), tok3(fk16), tok3(fv16), cache_k, cache_v, l, page_table, bias_t)
        xs = _out_proj(ret3.reshape(bd, GROUP_WIDTH), fox3.reshape(bd, GROUP_WIDTH), w_out, l, xs, mod, bd, GROUP_WIDTH)
        xs = _channel(xs, mod, l, bd, bd, p)
        outs[4].append(fk32.reshape(bd, 1, N_HEADS, HEAD_DIM))
        outs[5].append(fv32.reshape(bd, 1, N_HEADS, HEAD_DIM))
        outs[6].append(logf.reshape(bd, 1, -1))
        outs[7].append(s_new)
    return (xp.reshape(batch, seq, d), xs.reshape(bd, 1, d)) + tuple(jnp.stack(o) for o in outs)
```

```python
import functools

import jax
import jax.numpy as jnp
from jax import lax
from jax.experimental import pallas as pl
from jax.experimental.pallas import tpu as pltpu

f32, bf16 = jnp.float32, jnp.bfloat16

HEAD_DIM = 128
N_HEADS = 8
GROUP_WIDTH = N_HEADS * HEAD_DIM
RET_CHUNK = 128
PAGE_SIZE = 128
ROPE_BASE = 10000.0
N_EXPERTS = 16
EXPERTS_PER_GROUP = 4
EPS = 1e-6
LANES = 128
V7X_VMEM_LIMIT = 56 * 1024 * 1024
NEG_BIG = -1e30

_NT = (((1,), (1,)), ((), ()))
_TN = (((0,), (0,)), ((), ()))


def _params(n_axes):
    return pltpu.CompilerParams(dimension_semantics=("arbitrary",) * n_axes, vmem_limit_bytes=V7X_VMEM_LIMIT)


def _silu(x):
    return x * jax.nn.sigmoid(x)


def _log_sigmoid(x):
    return jnp.minimum(x, 0.0) - jnp.log1p(jnp.exp(-jnp.abs(x)))


def _adaln_body(c_ref, w_ref, b_ref, o_ref):
    s = _silu(c_ref[...]).astype(bf16)
    o_ref[...] = jnp.dot(s, w_ref[...].astype(bf16), preferred_element_type=f32) + b_ref[...]


def _adaln(c_all, w_ada, b_ada):
    depth, d, m6 = w_ada.shape
    nmod, r, tn = m6 // d, c_all.shape[0], 512
    nj = d // tn
    return pl.pallas_call(
        _adaln_body, grid=(depth, nmod, nj),
        in_specs=[pl.BlockSpec((r, d), lambda l, c, j: (0, 0)),
                  pl.BlockSpec((None, d, tn), lambda l, c, j: (l, 0, c * nj + j)),
                  pl.BlockSpec((None, 1, tn), lambda l, c, j: (l, 0, c * nj + j))],
        out_specs=pl.BlockSpec((None, None, r, tn), lambda l, c, j: (l, c, 0, j)),
        out_shape=jax.ShapeDtypeStruct((depth, nmod, r, d), f32),
        compiler_params=_params(3), name="adaln")(c_all, w_ada, b_ada.reshape(depth, 1, m6))


class _Mod:
    def __init__(self, mod, layer, n_sample, rows_per_seq):
        self.mod, self.layer, self.n_sample, self.rows_per_seq = mod, layer, n_sample, rows_per_seq

    def operand(self, chunk, tm, tn, row_of, col_of):
        depth, nmod, r, d = self.mod.shape
        l, bd, rps = self.layer, self.n_sample, self.rows_per_seq
        if rps is None:
            assert tm == bd
            return self.mod, pl.BlockSpec((None, None, tm, tn), lambda *g: (l, chunk, 0, col_of(*g)))
        return (self.mod.reshape(depth, nmod, r, 1, d),
                pl.BlockSpec((None, None, None, 1, tn), lambda *g: (l, chunk, bd + (row_of(*g) * tm) // rps, 0, col_of(*g))))


def _modulated(x_ref, g_ref, sc_ref, sh_ref):
    x = x_ref[...]
    y = x * lax.rsqrt(jnp.mean(x * x, axis=-1, keepdims=True) + EPS) * g_ref[...]
    return y * (1.0 + sc_ref[...]) + sh_ref[...]


def _modulate_body(x_ref, g_ref, sc_ref, sh_ref, o_ref):
    o_ref[...] = _modulated(x_ref, g_ref, sc_ref, sh_ref).astype(o_ref.dtype)


def _route(logits, b_router):
    scores = jax.nn.sigmoid(logits)
    biased = scores + b_router
    lane_i = lax.broadcasted_iota(jnp.int32, logits.shape, 1)
    lane = lane_i.astype(f32)
    grp = lane_i >> 2
    neg = -jnp.inf

    def top2(v):
        t1 = jnp.max(v, axis=-1, keepdims=True)
        i1 = jnp.min(jnp.where(v == t1, lane, float(LANES)), axis=-1, keepdims=True)
        v2 = jnp.where(lane == i1, neg, v)
        t2 = jnp.max(v2, axis=-1, keepdims=True)
        i2 = jnp.min(jnp.where(v2 == t2, lane, float(LANES)), axis=-1, keepdims=True)
        return t1, i1, t2, i2

    best, best_g = None, None
    for g in range(N_EXPERTS // EXPERTS_PER_GROUP):
        t1, _, t2, _ = top2(jnp.where(grp == g, biased, neg))
        gs = t1 + t2
        if best is None:
            best, best_g = gs, jnp.zeros(gs.shape, jnp.int32)
        else:
            take = gs > best
            best_g = jnp.where(take, g, best_g)
            best = jnp.where(take, gs, best)
    _, i1, _, i2 = top2(jnp.where(grp == best_g, biased, neg))
    s1 = jnp.sum(jnp.where(lane == i1, scores, 0.0), axis=-1, keepdims=True)
    s2 = jnp.sum(jnp.where(lane == i2, scores, 0.0), axis=-1, keepdims=True)
    den = s1 + s2
    return jnp.where(lane == i1, s1 / den, 0.0) + jnp.where(lane == i2, s2 / den, 0.0)


def _modulate_route_body(x_ref, g_ref, sc_ref, sh_ref, wr_ref, br_ref, o_ref, comb_ref):
    h = _modulated(x_ref, g_ref, sc_ref, sh_ref)
    o_ref[...] = h.astype(o_ref.dtype)
    logits = jnp.dot(h, wr_ref[...], precision=lax.Precision.HIGHEST, preferred_element_type=f32)
    comb_ref[...] = _route(logits, br_ref[...])


def _modulate(x, norm_g, mod, sc_chunk, sh_chunk, tm, router=None):
    n, d = x.shape
    row, col = (lambda i: i), (lambda i: 0)
    sc, sc_spec = mod.operand(sc_chunk, tm, d, row, col)
    sh, sh_spec = mod.operand(sh_chunk, tm, d, row, col)
    in_specs = [pl.BlockSpec((tm, d), lambda i: (i, 0)), pl.BlockSpec((1, d), lambda i: (0, 0)), sc_spec, sh_spec]
    args = [x, norm_g.reshape(1, d), sc, sh]
    out_specs = pl.BlockSpec((tm, d), lambda i: (i, 0))
    out_shape = jax.ShapeDtypeStruct((n, d), bf16)
    body = _modulate_body
    if router is not None:
        w_router_pad, b_router_pad = router
        in_specs += [pl.BlockSpec((d, LANES), lambda i: (0, 0)), pl.BlockSpec((1, LANES), lambda i: (0, 0))]
        args += [w_router_pad, b_router_pad]
        out_specs = [out_specs, pl.BlockSpec((tm, LANES), lambda i: (i, 0))]
        out_shape = [out_shape, jax.ShapeDtypeStruct((n, LANES), f32)]
        body = _modulate_route_body
    return pl.pallas_call(body, grid=(n // tm,), in_specs=in_specs, out_specs=out_specs, out_shape=out_shape,
                          compiler_params=_params(1), name="modulate")(*args)


def _proj_body(epilogue, n_extra, h_ref, w_ref, *rest):
    extra, outs, wb_ref = rest[:n_extra], rest[n_extra:-1], rest[-1]

    @pl.when(pl.program_id(1) == 0)
    def _():
        wb_ref[...] = w_ref[...].T.astype(bf16)

    acc = jnp.dot(h_ref[...], wb_ref[...], preferred_element_type=f32)
    epilogue(acc, extra, outs)


def _heads(width):
    return [slice(h * HEAD_DIM, (h + 1) * HEAD_DIM) for h in range(width // HEAD_DIM)]


def _epi_rope(acc, extra, outs):
    cos, sin = extra[0][...], extra[1][...]
    kscale = jnp.where(pl.program_id(0) == 1, HEAD_DIM ** -0.5, 1.0).astype(f32)
    for hs in _heads(acc.shape[1]):
        a = acc[:, hs]
        outs[0][:, hs] = ((a * cos + pltpu.roll(a, HEAD_DIM // 2, 1) * sin) * kscale).astype(outs[0].dtype)


def _epi_headnorm(acc, extra, outs):
    g, post = extra[0][...], extra[1][...]
    for hs in _heads(acc.shape[1]):
        a = acc[:, hs]
        y = a * lax.rsqrt(jnp.mean(a * a, axis=-1, keepdims=True) + EPS) * g
        for o in outs[:-1]:
            o[:, hs] = y.astype(o.dtype)
        outs[-1][:, hs] = (y * post).astype(outs[-1].dtype)


def _epi_plain(acc, extra, outs):
    for o in outs:
        o[...] = acc.astype(o.dtype)


def _epi_forget(acc, extra, outs):
    y = _log_sigmoid(acc + extra[0][...])
    outs[0][...] = y[:, :outs[0].shape[1]]


def _proj(h16, w_t, layer, col0, ncol, tn, tm, epilogue, extra, out_dtypes, out_width=None):
    n, k = h16.shape
    out_width = tn if out_width is None else out_width
    in_specs = [pl.BlockSpec((tm, k), lambda j, i: (i, 0)),
                pl.BlockSpec((None, tn, k), lambda j, i: (layer, col0 + j, 0))] + [s for _, s in extra]
    out_specs = [pl.BlockSpec((tm, out_width), lambda j, i: (i, j)) for _ in out_dtypes]
    out_shape = [jax.ShapeDtypeStruct((n, ncol * out_width), dt) for dt in out_dtypes]
    return pl.pallas_call(
        functools.partial(_proj_body, epilogue, len(extra)), grid=(ncol, n // tm),
        in_specs=in_specs, out_specs=out_specs, out_shape=out_shape,
        scratch_shapes=[pltpu.VMEM((k, tn), bf16)],
        compiler_params=_params(2), name="proj_" + epilogue.__name__[5:])(h16, w_t, *[a for a, _ in extra])


def _group_norm_gate(o, gain, gate):
    mu = jnp.mean(o, axis=-1, keepdims=True)
    var = jnp.mean(jnp.square(o - mu), axis=-1, keepdims=True)
    return (o - mu) * lax.rsqrt(var + EPS) * gain * _silu(gate)


def _ret_prompt_body(q_ref, k_ref, v_ref, g_ref, intra_ref, crossd_ref, kdec_ref, sdec_ref, gn_ref, o_ref, st_ref):
    @pl.when(pl.program_id(1) == 0)
    def _():
        st_ref[...] = jnp.zeros(st_ref.shape, f32)

    for h, hs in enumerate(_heads(q_ref.shape[1])):
        q, k, v = q_ref[:, hs], k_ref[:, hs], v_ref[:, hs]
        state = st_ref[h]
        scores = lax.dot_general(q, k, _NT, preferred_element_type=f32) * intra_ref[h]
        o = jnp.dot(scores.astype(bf16), v, preferred_element_type=f32)
        o = o + jnp.dot(q, state.astype(bf16), preferred_element_type=f32) * crossd_ref[h]
        k_dec = (k.astype(f32) * kdec_ref[h]).astype(bf16)
        st_ref[h] = state * sdec_ref[h] + lax.dot_general(k_dec, v, _TN, preferred_element_type=f32)
        o_ref[:, hs] = _group_norm_gate(o, gn_ref[:, hs], g_ref[:, hs].astype(f32)).astype(o_ref.dtype)


def _ret_tables():
    lg = jnp.log1p(-jnp.exp2(-5.0 - jnp.arange(N_HEADS, dtype=f32)))[:, None, None]
    t = jnp.arange(RET_CHUNK, dtype=f32)
    diff = t[None, :, None] - t[None, None, :]
    intra = jnp.where(diff >= 0, jnp.exp(lg * jnp.maximum(diff, 0.0)), 0.0)
    ones = jnp.ones((1, 1, HEAD_DIM), f32)
    crossd = jnp.exp(lg * (t[None, :, None] + 1.0)) * ones
    kdec = jnp.exp(lg * (RET_CHUNK - 1.0 - t[None, :, None])) * ones
    sdec = jnp.exp(lg * RET_CHUNK) * ones
    step_dec = jnp.exp(lg) * ones
    return intra, crossd, kdec, sdec, step_dec


def _ret_prompt(qk16, vg16, ret_norm_g, batch, tables):
    n = qk16.shape[0]
    nc = n // batch // RET_CHUNK
    intra, crossd, kdec, sdec, _ = tables
    row = lambda b, c: b * nc + c
    full3 = lambda a: pl.BlockSpec(a.shape, lambda b, c: (0, 0, 0))
    blk = lambda col: pl.BlockSpec((RET_CHUNK, GROUP_WIDTH), lambda b, c: (row(b, c), col))
    return pl.pallas_call(
        _ret_prompt_body, grid=(batch, nc),
        in_specs=[blk(0), blk(1), blk(0), blk(1), full3(intra), full3(crossd), full3(kdec), full3(sdec),
                  pl.BlockSpec((1, GROUP_WIDTH), lambda b, c: (0, 0))],
        out_specs=[blk(0), pl.BlockSpec((None, N_HEADS, HEAD_DIM, HEAD_DIM), lambda b, c: (b, 0, 0, 0))],
        out_shape=[jax.ShapeDtypeStruct((n, GROUP_WIDTH), bf16),
                   jax.ShapeDtypeStruct((batch, N_HEADS, HEAD_DIM, HEAD_DIM), f32)],
        compiler_params=_params(2), name="ret_prompt")(
            qk16, qk16, vg16, vg16, intra, crossd, kdec, sdec, ret_norm_g.reshape(1, GROUP_WIDTH))


def _block_diag(x):
    xt = jnp.tile(x, (1, N_HEADS))
    lane = lax.broadcasted_iota(jnp.int32, xt.shape, 1)
    row = lax.broadcasted_iota(jnp.int32, xt.shape, 0)
    return jnp.where((lane >> 7) == row, xt, 0.0)


RET_SAMPLE_SEQS = 4


def _ret_sample_body(q_ref, k_ref, v_ref, g_ref, st_ref, dec3_ref, dec2_ref, gn_ref, o_ref, sn_ref):
    for i in range(q_ref.shape[0]):
        q16, k16, v16 = q_ref[i].astype(bf16), k_ref[i].astype(bf16), v_ref[i].astype(bf16)
        q, k, v = q16.astype(f32), k16.astype(f32), v16.astype(f32)
        state = st_ref[i]
        cross = jnp.dot(_block_diag(q).astype(bf16), state.reshape(GROUP_WIDTH, HEAD_DIM).astype(bf16),
                        preferred_element_type=f32) * dec2_ref[...]
        qk = jnp.sum(q * k, axis=-1, keepdims=True).astype(bf16).astype(f32)
        o = qk * v + cross
        upd = lax.dot_general(_block_diag(k).astype(bf16), v16, _TN, preferred_element_type=f32)
        sn_ref[i] = state * dec3_ref[...] + upd.reshape(state.shape)
        o_ref[i] = _group_norm_gate(o, gn_ref[...], g_ref[i]).astype(o_ref.dtype)


def _ret_sample(q3, k3, v3, g3, state_ret, layer, ret_norm_g, step_dec):
    bd = q3.shape[0]
    nb = RET_SAMPLE_SEQS if bd % RET_SAMPLE_SEQS == 0 else 1
    tok = pl.BlockSpec((nb, N_HEADS, HEAD_DIM), lambda b: (b, 0, 0))
    return pl.pallas_call(
        _ret_sample_body, grid=(bd // nb,),
        in_specs=[tok, tok, tok, tok,
                  pl.BlockSpec((None, nb, N_HEADS, HEAD_DIM, HEAD_DIM), lambda b: (layer, b, 0, 0, 0)),
                  pl.BlockSpec(step_dec.shape, lambda b: (0, 0, 0)),
                  pl.BlockSpec((N_HEADS, HEAD_DIM), lambda b: (0, 0)),
                  pl.BlockSpec((N_HEADS, HEAD_DIM), lambda b: (0, 0))],
        out_specs=[tok, pl.BlockSpec((nb, N_HEADS, HEAD_DIM, HEAD_DIM), lambda b: (b, 0, 0, 0))],
        out_shape=[jax.ShapeDtypeStruct((bd, N_HEADS, HEAD_DIM), f32),
                   jax.ShapeDtypeStruct((bd, N_HEADS, HEAD_DIM, HEAD_DIM), f32)],
        compiler_params=_params(1), name="ret_sample")(
            q3, k3, v3, g3, state_ret, step_dec, step_dec.reshape(N_HEADS, HEAD_DIM),
            ret_norm_g.reshape(N_HEADS, HEAD_DIM))


def _cumsum_body(x_ref, o_ref, carry_ref):
    @pl.when(pl.program_id(1) == 0)
    def _():
        carry_ref[...] = jnp.zeros(carry_ref.shape, f32)

    tb = x_ref.shape[1]
    upper = (lax.broadcasted_iota(jnp.int32, (tb, tb), 0) <= lax.broadcasted_iota(jnp.int32, (tb, tb), 1)).astype(f32)
    cum = jnp.dot(x_ref[...], upper, precision=lax.Precision.HIGHEST, preferred_element_type=f32) + carry_ref[:, :1]
    o_ref[...] = cum
    carry_ref[...] = jnp.broadcast_to(cum[:, tb - 1:tb], carry_ref.shape)


def _cumsum_lanes(x_t, tb):
    b, h, s = x_t.shape
    spec = pl.BlockSpec((None, h, tb), lambda i, j: (i, 0, j))
    return pl.pallas_call(_cumsum_body, grid=(b, s // tb), in_specs=[spec], out_specs=spec,
                          out_shape=jax.ShapeDtypeStruct(x_t.shape, f32),
                          scratch_shapes=[pltpu.VMEM((h, LANES), f32)],
                          compiler_params=_params(2), name="forget_cumsum")(x_t)


LOG2E = 1.4426950408889634
HEADS_PER_STEP = 2


def _forget_tail(f_sel, first_lane, sign, ones_lane):
    x = f_sel * (sign * LOG2E)
    hi = x.astype(bf16).astype(f32)
    rest = x - hi
    mid = rest.astype(bf16).astype(f32)
    lo = (rest - mid).astype(bf16).astype(f32)
    lane = lax.broadcasted_iota(jnp.int32, (f_sel.shape[0], HEAD_DIM), 1)
    ones = jnp.where((lane >= ones_lane) & (lane < ones_lane + 3), 1.0, 0.0)
    tail = jnp.where(lane == first_lane, hi, jnp.where(lane == first_lane + 1, mid,
                                                      jnp.where(lane == first_lane + 2, lo, ones)))
    return tail.astype(bf16)


def _select_head(f_block, head):
    lane = lax.broadcasted_iota(jnp.int32, f_block.shape, 1)
    return jnp.sum(jnp.where(lane == head, f_block, 0.0), axis=-1, keepdims=True)


def _fox_prompt_body(q_ref, k_ref, v_ref, fq_ref, fk_ref, o_ref, qaug_ref, kaug_ref, m_ref, l_ref, acc_ref):
    pair, qi = pl.program_id(1), pl.program_id(2)
    tq = q_ref.shape[0]
    n_sub = q_ref.shape[1] // HEAD_DIM
    seq = k_ref.shape[0]

    @pl.when(qi == 0)
    def _():
        def fill(c, carry):
            rows = pl.ds(pl.multiple_of(c * tq, tq), tq)
            for hh, hs in enumerate(_heads(q_ref.shape[1])):
                kaug_ref[hh, rows, :HEAD_DIM] = k_ref[rows, hs]
                kaug_ref[hh, rows, HEAD_DIM:] = _forget_tail(_select_head(fk_ref[rows, :], pair * n_sub + hh), 3, -1.0, 0)
            return carry
        lax.fori_loop(0, seq // tq, fill, 0)

    for hh, hs in enumerate(_heads(q_ref.shape[1])):
        qaug_ref[hh, :, :HEAD_DIM] = q_ref[:, hs]
        qaug_ref[hh, :, HEAD_DIM:] = _forget_tail(_select_head(fq_ref[...], pair * n_sub + hh), 0, 1.0, 3)
    m_ref[...] = jnp.full(m_ref.shape, NEG_BIG, f32)
    l_ref[...] = jnp.zeros(l_ref.shape, f32)
    acc_ref[...] = jnp.zeros(acc_ref.shape, f32)

    def step(j, diagonal):
        rows = pl.ds(pl.multiple_of(j * tq, tq), tq)
        for hh, hs in enumerate(_heads(q_ref.shape[1])):
            s = lax.dot_general(qaug_ref[hh], kaug_ref[hh, rows, :], _NT, preferred_element_type=f32)
            if diagonal:
                r = lax.broadcasted_iota(jnp.int32, s.shape, 0)
                c = lax.broadcasted_iota(jnp.int32, s.shape, 1)
                s = jnp.where(c <= r, s, -jnp.inf)
            m_old = m_ref[hh]
            m_new = jnp.maximum(m_old, jnp.max(s, axis=-1, keepdims=True))
            alpha = jnp.exp2(m_old - m_new)
            p = jnp.exp2(s - jnp.tile(m_new, (1, tq // LANES)))
            l_ref[hh] = alpha * l_ref[hh] + jnp.sum(p, axis=-1, keepdims=True)
            acc_ref[hh] = alpha * acc_ref[hh] + jnp.dot(p.astype(bf16), v_ref[rows, hs], preferred_element_type=f32)
            m_ref[hh] = m_new

    def off_diagonal(j, carry):
        step(j, False)
        return carry

    lax.fori_loop(0, qi, off_diagonal, 0)
    step(qi, True)
    for hh, hs in enumerate(_heads(q_ref.shape[1])):
        o_ref[:, hs] = (acc_ref[hh] / l_ref[hh]).astype(o_ref.dtype)


def _fox_prompt(q16, k16, v16, f_cols, batch, tq):
    n = q16.shape[0]
    s = n // batch
    nq = s // tq
    width = HEADS_PER_STEP * HEAD_DIM
    kv = lambda a: a.reshape(batch, s, GROUP_WIDTH)
    kv_spec = pl.BlockSpec((None, s, width), lambda b, h, i: (b, 0, h))
    q_spec = pl.BlockSpec((tq, width), lambda b, h, i: (b * nq + i, h))
    sub = (HEADS_PER_STEP, tq, HEAD_DIM)
    return pl.pallas_call(
        _fox_prompt_body, grid=(batch, N_HEADS // HEADS_PER_STEP, nq),
        in_specs=[q_spec, kv_spec, kv_spec,
                  pl.BlockSpec((None, tq, N_HEADS), lambda b, h, i: (b, i, 0)),
                  pl.BlockSpec((None, s, N_HEADS), lambda b, h, i: (b, 0, 0))],
        out_specs=q_spec,
        out_shape=jax.ShapeDtypeStruct((n, GROUP_WIDTH), bf16),
        scratch_shapes=[pltpu.VMEM((HEADS_PER_STEP, tq, 2 * HEAD_DIM), bf16),
                        pltpu.VMEM((HEADS_PER_STEP, s, 2 * HEAD_DIM), bf16),
                        pltpu.VMEM(sub, f32), pltpu.VMEM(sub, f32), pltpu.VMEM(sub, f32)],
        compiler_params=_params(3), name="fox_prompt")(q16, kv(k16), kv(v16), f_cols, f_cols)


FLAT = PAGE_SIZE * N_HEADS
PAGES_PER_STEP = 8


def _dot_select(y, m01):
    hi = y.astype(bf16)
    rest = y - hi.astype(f32)
    mid = rest.astype(bf16)
    lo = (rest - mid.astype(f32)).astype(bf16)
    return sum(jnp.dot(piece, m01, preferred_element_type=f32) for piece in (hi, mid, lo))


def _fold_heads(z, own_head):
    z = jnp.where(own_head, z, 0.0)
    return jnp.sum(z.reshape(z.shape[0] // N_HEADS, N_HEADS, FLAT), axis=1)


def _forget_pages_body(x_ref, excl_ref, tot_ref):
    x = x_ref[...]
    t = x.shape[1]
    later = (lax.broadcasted_iota(jnp.int32, (t, t), 0) > lax.broadcasted_iota(jnp.int32, (t, t), 1)).astype(bf16)
    spread = (lax.broadcasted_iota(jnp.int32, (t, FLAT), 1) >> 3
              == lax.broadcasted_iota(jnp.int32, (t, FLAT), 0)).astype(bf16)
    excl = _dot_select(x, later)
    own_head = ((lax.broadcasted_iota(jnp.int32, (x.shape[0], FLAT), 1) & (N_HEADS - 1))
                == (lax.broadcasted_iota(jnp.int32, (x.shape[0], FLAT), 0) & (N_HEADS - 1)))
    excl_ref[...] = _fold_heads(_dot_select(excl, spread), own_head)
    tot_ref[...] = _fold_heads(jnp.broadcast_to(jnp.sum(x, axis=-1, keepdims=True), own_head.shape), own_head)


def _forget_pages(cache_logf):
    depth, n_phys, page, heads = cache_logf.shape
    assert page == PAGE_SIZE and heads == N_HEADS
    n = depth * n_phys
    pp = next((c for c in (64, 32, 16, 8) if n % c == 0), n)
    rows = cache_logf.transpose(0, 1, 3, 2).reshape(n * heads, page)
    out = jax.ShapeDtypeStruct((n, FLAT), f32)
    excl, tot = pl.pallas_call(
        _forget_pages_body, grid=(n // pp,),
        in_specs=[pl.BlockSpec((pp * heads, page), lambda i: (i, 0))],
        out_specs=[pl.BlockSpec((pp, FLAT), lambda i: (i, 0))] * 2, out_shape=[out, out],
        compiler_params=_params(1), name="forget_pages")(rows)
    return excl.reshape(depth, n_phys, 1, FLAT), tot.reshape(depth, n_phys, 1, FLAT)


def _fox_sample_body(n_group, pt_ref, q_ref, kn_ref, vn_ref, lnew_ref, *rest):
    k_refs, v_refs = rest[:n_group], rest[n_group:2 * n_group]
    excl_refs, tot_refs = rest[2 * n_group:3 * n_group], rest[3 * n_group:4 * n_group]
    o_ref, carry_ref, m_ref, l_ref, acc_ref = rest[4 * n_group:]
    step = pl.program_id(1)
    scale = HEAD_DIM ** -0.5

    @pl.when(step == 0)
    def _():
        carry_ref[...] = lnew_ref[...]
        m_ref[...] = jnp.full(m_ref.shape, NEG_BIG, f32)
        l_ref[...] = jnp.zeros(l_ref.shape, f32)
        acc_ref[...] = jnp.zeros(acc_ref.shape, f32)

    q16 = q_ref[...].astype(bf16)
    own_head = ((lax.broadcasted_iota(jnp.int32, (N_HEADS, FLAT), 1) & (N_HEADS - 1))
                == lax.broadcasted_iota(jnp.int32, (N_HEADS, FLAT), 0))
    carry = carry_ref[...]
    scores = []
    for j in range(n_group):
        kf = k_refs[j][...].reshape(FLAT, HEAD_DIM).astype(bf16)
        s = lax.dot_general(q16, kf, _NT, preferred_element_type=f32) * scale + (carry + excl_refs[j][...])
        scores.append(jnp.where(own_head, s, -jnp.inf))
        carry = carry + tot_refs[j][...]
    carry_ref[...] = carry

    top = scores[0]
    for s in scores[1:]:
        top = jnp.maximum(top, s)
    m_old = m_ref[...]
    m_new = jnp.maximum(m_old, jnp.max(top, axis=-1, keepdims=True))
    alpha = jnp.exp(m_old - m_new)
    l_new = alpha * l_ref[...]
    acc = alpha * acc_ref[...]
    for j in range(n_group):
        pe = jnp.exp(scores[j] - m_new)
        l_new = l_new + jnp.sum(pe, axis=-1, keepdims=True)
        vf = v_refs[j][...].reshape(FLAT, HEAD_DIM).astype(bf16)
        acc = acc + jnp.dot(pe.astype(bf16), vf, preferred_element_type=f32)
    m_ref[...] = m_new
    l_ref[...] = l_new
    acc_ref[...] = acc

    @pl.when(step == pl.num_programs(1) - 1)
    def _():
        rounded = lambda ref: ref[...].astype(bf16).astype(f32)
        s_new = jnp.sum(rounded(q_ref) * rounded(kn_ref), axis=-1, keepdims=True) * scale
        m_fin = jnp.maximum(m_new, s_new)
        beta = jnp.exp(m_new - m_fin)
        p_new = jnp.exp(s_new - m_fin)
        num = beta * acc + p_new.astype(bf16).astype(f32) * rounded(vn_ref)
        o_ref[...] = (num / (beta * l_new + p_new)).astype(o_ref.dtype)


def _fox_sample(q3, kn3, vn3, logf_new, cache_k, cache_v, excl_flat, tot_flat, layer, page_table):
    bd, n_pages = page_table.shape
    g = PAGES_PER_STEP if n_pages % PAGES_PER_STEP == 0 else 1
    lnew_flat = jnp.tile(logf_new, (1, PAGE_SIZE)).reshape(bd, 1, FLAT)
    tok = pl.BlockSpec((None, N_HEADS, HEAD_DIM), lambda b, s, pt: (b, 0, 0))

    def page_of(j):
        return lambda b, s, pt: pt[b, n_pages - 1 - (s * g + j)]

    def kv_spec(j):
        pg = page_of(j)
        return pl.BlockSpec((None, None, PAGE_SIZE, N_HEADS, HEAD_DIM), lambda b, s, pt: (layer, pg(b, s, pt), 0, 0, 0))

    def flat_spec(j):
        pg = page_of(j)
        return pl.BlockSpec((None, None, 1, FLAT), lambda b, s, pt: (layer, pg(b, s, pt), 0, 0))

    groups = [[kv_spec(j) for j in range(g)], [kv_spec(j) for j in range(g)],
              [flat_spec(j) for j in range(g)], [flat_spec(j) for j in range(g)]]
    grid_spec = pltpu.PrefetchScalarGridSpec(
        num_scalar_prefetch=1, grid=(bd, n_pages // g),
        in_specs=[tok, tok, tok, pl.BlockSpec((None, 1, FLAT), lambda b, s, pt: (b, 0, 0))] + sum(groups, []),
        out_specs=tok,
        scratch_shapes=[pltpu.VMEM((1, FLAT), f32), pltpu.VMEM((N_HEADS, 1), f32), pltpu.VMEM((N_HEADS, 1), f32),
                        pltpu.VMEM((N_HEADS, HEAD_DIM), f32)])
    return pl.pallas_call(
        functools.partial(_fox_sample_body, g), grid_spec=grid_spec,
        out_shape=jax.ShapeDtypeStruct((bd, N_HEADS, HEAD_DIM), f32),
        compiler_params=_params(2), name="fox_sample")(
            page_table, q3, kn3, vn3, lnew_flat, *([cache_k] * g + [cache_v] * g + [excl_flat] * g + [tot_flat] * g))


def _out_proj_body(ret_ref, fox_ref, w_ref, x_ref, ga_ref, o_ref, wb_ref):
    @pl.when(pl.program_id(1) == 0)
    def _():
        wb_ref[...] = w_ref[...].astype(bf16)

    half = ret_ref.shape[1]
    y = jnp.dot(ret_ref[...].astype(bf16), wb_ref[:half, :], preferred_element_type=f32)
    y = y + jnp.dot(fox_ref[...].astype(bf16), wb_ref[half:, :], preferred_element_type=f32)
    o_ref[...] = x_ref[...] + ga_ref[...] * y


def _out_proj(ret16, fox16, w_out, layer, x, mod, tm, tn):
    n, d = x.shape
    k = w_out.shape[1]
    ga, ga_spec = mod.operand(2, tm, tn, lambda j, i: i, lambda j, i: j)
    act = pl.BlockSpec((tm, ret16.shape[1]), lambda j, i: (i, 0))
    xs = pl.BlockSpec((tm, tn), lambda j, i: (i, j))
    return pl.pallas_call(
        _out_proj_body, grid=(d // tn, n // tm),
        in_specs=[act, act, pl.BlockSpec((None, k, tn), lambda j, i: (layer, 0, j)), xs, ga_spec],
        out_specs=xs, out_shape=jax.ShapeDtypeStruct((n, d), f32),
        scratch_shapes=[pltpu.VMEM((k, tn), bf16)],
        compiler_params=_params(2), name="out_proj")(ret16, fox16, w_out, x, ga)


def _moe_body(h_ref, comb_ref, wg_ref, wu_ref, wd_ref, x_ref, gm_ref, o_ref, acc_ref):
    e = pl.program_id(1)

    @pl.when(e == 0)
    def _():
        acc_ref[...] = jnp.zeros(acc_ref.shape, f32)

    h = h_ref[...]
    g = jnp.dot(h, wg_ref[...], preferred_element_type=f32)
    u = jnp.dot(h, wu_ref[...], preferred_element_type=f32)
    y = jnp.dot((_silu(g) * u).astype(bf16), wd_ref[...], preferred_element_type=f32)
    lane = lax.broadcasted_iota(jnp.int32, comb_ref.shape, 1)
    w = jnp.sum(jnp.where(lane == e, comb_ref[...], 0.0), axis=-1, keepdims=True)
    acc_ref[...] += w * y

    @pl.when(e == pl.num_programs(1) - 1)
    def _():
        o_ref[...] = x_ref[...] + gm_ref[...] * acc_ref[...]


def _moe(h16, comb, wg16, wu16, wd16, layer, x, mod, tm):
    n, d = x.shape
    n_exp, _, de = wg16.shape[1:]
    gm, gm_spec = mod.operand(5, tm, d, lambda i, e: i, lambda i, e: 0)
    rows = lambda w: pl.BlockSpec((tm, w), lambda i, e: (i, 0))
    return pl.pallas_call(
        _moe_body, grid=(n // tm, n_exp),
        in_specs=[rows(d), rows(LANES),
                  pl.BlockSpec((None, None, d, de), lambda i, e: (layer, e, 0, 0)),
                  pl.BlockSpec((None, None, d, de), lambda i, e: (layer, e, 0, 0)),
                  pl.BlockSpec((None, None, de, d), lambda i, e: (layer, e, 0, 0)),
                  rows(d), gm_spec],
        out_specs=rows(d), out_shape=jax.ShapeDtypeStruct((n, d), f32),
        scratch_shapes=[pltpu.VMEM((tm, d), f32)],
        compiler_params=_params(2), name="moe")(h16, comb, wg16, wu16, wd16, x, gm)


def _rope_tables(positions):
    half = HEAD_DIM // 2
    inv_freq = ROPE_BASE ** (-jnp.arange(half, dtype=f32) / half)
    ang = positions.astype(f32)[:, None] * inv_freq[None, :]
    cos, sin = jnp.cos(ang), jnp.sin(ang)
    return jnp.concatenate([cos, cos], axis=-1), jnp.concatenate([-sin, sin], axis=-1)


def _mixer_in(x, mod, layer, tm, pos_tables, pos_tiles, p, lo, q_scale):
    h16 = _modulate(x, p["norm_mix_g"][layer], mod, 1, 0, tm)
    cos, sin = pos_tables
    pos_spec = pl.BlockSpec((tm, HEAD_DIM), lambda j, i: (i % pos_tiles, 0))
    w_in = p["w_in_t"]
    gw = GROUP_WIDTH
    (qk16,) = _proj(h16, w_in, layer, 0, 2, gw, tm, _epi_rope, [(cos, pos_spec), (sin, pos_spec)], [lo])
    (vg16,) = _proj(h16, w_in, layer, 2, 2, gw, tm, _epi_plain, [], [lo])
    row = lambda a: (a.reshape(1, HEAD_DIM), pl.BlockSpec((1, HEAD_DIM), lambda j, i: (0, 0)))
    const = lambda c: row(jnp.full((HEAD_DIM,), c, f32))
    (fq16,) = _proj(h16, w_in, layer, 4, 1, gw, tm, _epi_headnorm, [row(p["q_norm_g"][layer]), const(q_scale)], [lo])
    fk32, fk16 = _proj(h16, w_in, layer, 5, 1, gw, tm, _epi_headnorm, [row(p["k_norm_g"][layer]), const(1.0)], [f32, lo])
    fv32, fv16 = _proj(h16, w_in, layer, 6, 1, gw, tm, _epi_plain, [], [f32, lo])
    n_forget = p["b_forget"].shape[1]
    w_f = jnp.pad(w_in[layer, 7 * gw:, :], ((0, LANES - n_forget), (0, 0)))[None]
    b_f = jnp.pad(p["b_forget"][layer], (0, LANES - n_forget)).reshape(1, LANES)
    (logf,) = _proj(h16, w_f, 0, 0, 1, LANES, tm, _epi_forget,
                    [(b_f, pl.BlockSpec((1, LANES), lambda j, i: (0, 0)))], [f32], out_width=n_forget)
    return qk16, vg16, fq16, fk32, fk16, fv32, fv16, logf


def _channel(x, mod, layer, tm, tm_moe, p):
    h16, comb = _modulate(x, p["norm_ffn_g"][layer], mod, 4, 3, tm, router=(p["w_router_pad"], p["b_router_pad"]))
    return _moe(h16, comb, p["wg16"], p["wu16"], p["wd16"], layer, x, mod, tm_moe)


def kernel(x_prompt, x_sample, cache_k, cache_v, cache_logf, state_ret, page_table, c_prompt, c_sample, w_ada, b_ada,
           norm_mix_g, norm_ffn_g, w_in, b_forget, q_norm_g, k_norm_g, ret_norm_g, w_out, w_router, b_router,
           w_gate, w_up, w_down):
    batch, seq, d = x_prompt.shape
    bd, t_new, _ = x_sample.shape
    assert t_new == 1, "one new token per sampled sequence"
    assert bd % 8 == 0 and d == 2 * GROUP_WIDTH
    depth = w_in.shape[0]
    n_pages = page_table.shape[1]
    past_len = n_pages * PAGE_SIZE
    n_p = batch * seq
    tm_p = min(512, seq)
    tq = min(512, seq)

    p = dict(norm_mix_g=norm_mix_g, norm_ffn_g=norm_ffn_g, w_in_t=w_in.transpose(0, 2, 1), b_forget=b_forget,
             q_norm_g=q_norm_g,
             k_norm_g=k_norm_g,
             w_router_pad=jnp.pad(w_router, ((0, 0), (0, LANES - w_router.shape[1]))),
             b_router_pad=jnp.pad(b_router, (0, LANES - b_router.shape[0])).reshape(1, LANES),
             wg16=w_gate.astype(bf16), wu16=w_up.astype(bf16), wd16=w_down.astype(bf16))

    r_rows = -(-(bd + batch) // 8) * 8
    c_all = jnp.concatenate([c_sample, c_prompt, jnp.zeros((r_rows - bd - batch, d), f32)], axis=0)
    mod_all = _adaln(c_all, w_ada, b_ada)

    excl_flat, tot_flat = _forget_pages(cache_logf)
    tables = _ret_tables()
    rope_p = _rope_tables(jnp.arange(seq))
    rope_s = _rope_tables(jnp.full((bd,), past_len))

    xp = x_prompt.reshape(n_p, d)
    xs = x_sample.reshape(bd, d)
    outs = [[] for _ in range(8)]
    tok3 = lambda a: a.reshape(bd, N_HEADS, HEAD_DIM)
    for l in range(depth):
        mod = _Mod(mod_all, l, bd, seq)
        qk16, vg16, fq16, fk32, fk16, fv32, fv16, logf = _mixer_in(xp, mod, l, tm_p, rope_p, seq // tm_p, p, bf16,
                                                                       LOG2E * HEAD_DIM ** -0.5)
        ret16, s_fin = _ret_prompt(qk16, vg16, ret_norm_g[l], batch, tables)
        logf_b = logf.reshape(batch, seq, -1)
        f_rows = _cumsum_lanes(logf_b.transpose(0, 2, 1), tq)
        fox16 = _fox_prompt(fq16, fk16, fv16, f_rows.transpose(0, 2, 1), batch, tq)
        xp = _out_proj(ret16, fox16, w_out, l, xp, mod, tm_p, GROUP_WIDTH)
        xp = _channel(xp, mod, l, tm_p, tm_p, p)
        outs[0].append(fk32.reshape(batch, seq, N_HEADS, HEAD_DIM))
        outs[1].append(fv32.reshape(batch, seq, N_HEADS, HEAD_DIM))
        outs[2].append(logf_b)
        outs[3].append(s_fin)
        mod = _Mod(mod_all, l, bd, None)
        qk16, vg16, fq16, fk32, fk16, fv32, fv16, logf = _mixer_in(xs, mod, l, bd, rope_s, 1, p, f32, 1.0)
        ret3, s_new = _ret_sample(tok3(qk16[:, :GROUP_WIDTH]), tok3(qk16[:, GROUP_WIDTH:]), tok3(vg16[:, :GROUP_WIDTH]),
                                  tok3(vg16[:, GROUP_WIDTH:]), state_ret, l, ret_norm_g[l], tables[4])
        fox3 = _fox_sample(tok3(fq16), tok3(fk16), tok3(fv16), logf, cache_k, cache_v, excl_flat, tot_flat, l, page_table)
        xs = _out_proj(ret3.reshape(bd, GROUP_WIDTH), fox3.reshape(bd, GROUP_WIDTH), w_out, l, xs, mod, bd, GROUP_WIDTH)
        xs = _channel(xs, mod, l, bd, bd, p)
        outs[4].append(fk32.reshape(bd, 1, N_HEADS, HEAD_DIM))
        outs[5].append(fv32.reshape(bd, 1, N_HEADS, HEAD_DIM))
        outs[6].append(logf.reshape(bd, 1, -1))
        outs[7].append(s_new)
    return (xp.reshape(batch, seq, d), xs.reshape(bd, 1, d)) + tuple(jnp.stack(o) for o in outs)
```

```python
import functools

import jax
import jax.numpy as jnp
from jax import lax
from jax.experimental import pallas as pl
from jax.experimental.pallas import tpu as pltpu

f32, bf16 = jnp.float32, jnp.bfloat16

HEAD_DIM = 128
N_HEADS = 8
GROUP_WIDTH = N_HEADS * HEAD_DIM
RET_CHUNK = 128
PAGE_SIZE = 128
ROPE_BASE = 10000.0
N_EXPERTS = 16
EXPERTS_PER_GROUP = 4
EPS = 1e-6
LANES = 128
SUBLANES = 8
V7X_VMEM_LIMIT = 56 * 1024 * 1024
NEG_BIG = -1e30

_NT = (((1,), (1,)), ((), ()))
_TN = (((0,), (0,)), ((), ()))


def _params(n_axes):
    return pltpu.CompilerParams(dimension_semantics=("arbitrary",) * n_axes, vmem_limit_bytes=V7X_VMEM_LIMIT)


def _silu(x):
    return x * jax.nn.sigmoid(x)


def _log_sigmoid(x):
    return jnp.minimum(x, 0.0) - jnp.log1p(jnp.exp(-jnp.abs(x)))


def _adaln_body(c_ref, w_ref, b_ref, o_ref):
    s = _silu(c_ref[...]).astype(bf16)
    o_ref[...] = jnp.dot(s, w_ref[...].astype(bf16), preferred_element_type=f32) + b_ref[...]


def _adaln(c_all, w_ada, b_ada):
    depth, d, m6 = w_ada.shape
    nmod, r, tn = m6 // d, c_all.shape[0], 512
    nj = d // tn
    return pl.pallas_call(
        _adaln_body, grid=(depth, nmod, nj),
        in_specs=[pl.BlockSpec((r, d), lambda l, c, j: (0, 0)),
                  pl.BlockSpec((None, d, tn), lambda l, c, j: (l, 0, c * nj + j)),
                  pl.BlockSpec((None, 1, tn), lambda l, c, j: (l, 0, c * nj + j))],
        out_specs=pl.BlockSpec((None, None, r, tn), lambda l, c, j: (l, c, 0, j)),
        out_shape=jax.ShapeDtypeStruct((depth, nmod, r, d), f32),
        compiler_params=_params(3), name="adaln")(c_all, w_ada, b_ada.reshape(depth, 1, m6))


class _Mod:
    def __init__(self, mod, layer, n_sample, rows_per_seq):
        self.mod, self.layer, self.n_sample, self.rows_per_seq = mod, layer, n_sample, rows_per_seq

    def operand(self, chunk, tm, tn, row_of, col_of):
        depth, nmod, r, d = self.mod.shape
        l, bd, rps = self.layer, self.n_sample, self.rows_per_seq
        if rps is None:
            assert tm == bd
            return self.mod, pl.BlockSpec((None, None, tm, tn), lambda *g: (l, chunk, 0, col_of(*g)))
        return (self.mod.reshape(depth, nmod, r, 1, d),
                pl.BlockSpec((None, None, None, 1, tn), lambda *g: (l, chunk, bd + (row_of(*g) * tm) // rps, 0, col_of(*g))))


def _modulated(x_ref, g_ref, sc_ref, sh_ref):
    x = x_ref[...]
    y = x * lax.rsqrt(jnp.mean(x * x, axis=-1, keepdims=True) + EPS) * g_ref[...]
    return y * (1.0 + sc_ref[...]) + sh_ref[...]


def _modulate_body(x_ref, g_ref, sc_ref, sh_ref, o_ref):
    o_ref[...] = _modulated(x_ref, g_ref, sc_ref, sh_ref).astype(o_ref.dtype)


def _route(logits, b_router):
    scores = jax.nn.sigmoid(logits)
    biased = scores + b_router
    lane_i = lax.broadcasted_iota(jnp.int32, logits.shape, 1)
    lane = lane_i.astype(f32)
    grp = lane_i >> 2
    neg = -jnp.inf

    def top2(v):
        t1 = jnp.max(v, axis=-1, keepdims=True)
        i1 = jnp.min(jnp.where(v == t1, lane, float(LANES)), axis=-1, keepdims=True)
        v2 = jnp.where(lane == i1, neg, v)
        t2 = jnp.max(v2, axis=-1, keepdims=True)
        i2 = jnp.min(jnp.where(v2 == t2, lane, float(LANES)), axis=-1, keepdims=True)
        return t1, i1, t2, i2

    best, best_g = None, None
    for g in range(N_EXPERTS // EXPERTS_PER_GROUP):
        t1, _, t2, _ = top2(jnp.where(grp == g, biased, neg))
        gs = t1 + t2
        if best is None:
            best, best_g = gs, jnp.zeros(gs.shape, jnp.int32)
        else:
            take = gs > best
            best_g = jnp.where(take, g, best_g)
            best = jnp.where(take, gs, best)
    _, i1, _, i2 = top2(jnp.where(grp == best_g, biased, neg))
    s1 = jnp.sum(jnp.where(lane == i1, scores, 0.0), axis=-1, keepdims=True)
    s2 = jnp.sum(jnp.where(lane == i2, scores, 0.0), axis=-1, keepdims=True)
    den = s1 + s2
    return jnp.where(lane == i1, s1 / den, 0.0) + jnp.where(lane == i2, s2 / den, 0.0), best_g


def _modulate_route_body(x_ref, g_ref, sc_ref, sh_ref, wr_ref, br_ref, o_ref, comb_ref):
    h = _modulated(x_ref, g_ref, sc_ref, sh_ref)
    d = h.shape[1]
    o_ref[:, :d] = h.astype(o_ref.dtype)
    logits = jnp.dot(h, wr_ref[...], precision=lax.Precision.HIGHEST, preferred_element_type=f32)
    comb, group = _route(logits, br_ref[...])
    lane = lax.broadcasted_iota(jnp.int32, comb.shape, 1)
    comb = jnp.where(lane == N_EXPERTS, group.astype(f32), comb)
    comb_ref[...] = comb
    if o_ref.shape[1] > d:
        o_ref[:, d:] = comb


def _modulate(x, norm_g, mod, sc_chunk, sh_chunk, tm, router=None, pack_weights=False):
    n, d = x.shape
    row, col = (lambda i: i), (lambda i: 0)
    sc, sc_spec = mod.operand(sc_chunk, tm, d, row, col)
    sh, sh_spec = mod.operand(sh_chunk, tm, d, row, col)
    in_specs = [pl.BlockSpec((tm, d), lambda i: (i, 0)), pl.BlockSpec((1, d), lambda i: (0, 0)), sc_spec, sh_spec]
    args = [x, norm_g.reshape(1, d), sc, sh]
    out_specs = pl.BlockSpec((tm, d), lambda i: (i, 0))
    out_shape = jax.ShapeDtypeStruct((n, d), bf16)
    body = _modulate_body
    if router is not None:
        w_router_pad, b_router_pad = router
        in_specs += [pl.BlockSpec((d, LANES), lambda i: (0, 0)), pl.BlockSpec((1, LANES), lambda i: (0, 0))]
        args += [w_router_pad, b_router_pad]
        width = d + LANES if pack_weights else d
        out_specs = [pl.BlockSpec((tm, width), lambda i: (i, 0)), pl.BlockSpec((tm, LANES), lambda i: (i, 0))]
        out_shape = [jax.ShapeDtypeStruct((n, width), f32 if pack_weights else bf16),
                     jax.ShapeDtypeStruct((n, LANES), f32)]
        body = _modulate_route_body
    return pl.pallas_call(body, grid=(n // tm,), in_specs=in_specs, out_specs=out_specs, out_shape=out_shape,
                          compiler_params=_params(1), name="modulate")(*args)


def _proj_body(epilogue, n_extra, h_ref, w_ref, *rest):
    extra, outs, wb_ref = rest[:n_extra], rest[n_extra:-1], rest[-1]

    @pl.when(pl.program_id(1) == 0)
    def _():
        wb_ref[...] = w_ref[...].T.astype(bf16)

    acc = jnp.dot(h_ref[...], wb_ref[...], preferred_element_type=f32)
    epilogue(acc, extra, outs)


def _heads(width):
    return [slice(h * HEAD_DIM, (h + 1) * HEAD_DIM) for h in range(width // HEAD_DIM)]


def _epi_rope(acc, extra, outs):
    cos, sin = extra[0][...], extra[1][...]
    kscale = jnp.where(pl.program_id(0) == 1, HEAD_DIM ** -0.5, 1.0).astype(f32)
    for hs in _heads(acc.shape[1]):
        a = acc[:, hs]
        outs[0][:, hs] = ((a * cos + pltpu.roll(a, HEAD_DIM // 2, 1) * sin) * kscale).astype(outs[0].dtype)


def _epi_headnorm(acc, extra, outs):
    g, post = extra[0][...], extra[1][...]
    for hs in _heads(acc.shape[1]):
        a = acc[:, hs]
        y = a * lax.rsqrt(jnp.mean(a * a, axis=-1, keepdims=True) + EPS) * g
        for o in outs[:-1]:
            o[:, hs] = y.astype(o.dtype)
        outs[-1][:, hs] = (y * post).astype(outs[-1].dtype)


def _epi_plain(acc, extra, outs):
    for o in outs:
        o[...] = acc.astype(o.dtype)


def _epi_forget(acc, extra, outs):
    y = _log_sigmoid(acc + extra[0][...])
    outs[0][...] = y[:, :outs[0].shape[1]]


def _proj(h16, w_t, layer, col0, ncol, tn, tm, epilogue, extra, out_dtypes, out_width=None):
    n, k = h16.shape
    out_width = tn if out_width is None else out_width
    in_specs = [pl.BlockSpec((tm, k), lambda j, i: (i, 0)),
                pl.BlockSpec((None, tn, k), lambda j, i: (layer, col0 + j, 0))] + [s for _, s in extra]
    out_specs = [pl.BlockSpec((tm, out_width), lambda j, i: (i, j)) for _ in out_dtypes]
    out_shape = [jax.ShapeDtypeStruct((n, ncol * out_width), dt) for dt in out_dtypes]
    return pl.pallas_call(
        functools.partial(_proj_body, epilogue, len(extra)), grid=(ncol, n // tm),
        in_specs=in_specs, out_specs=out_specs, out_shape=out_shape,
        scratch_shapes=[pltpu.VMEM((k, tn), bf16)],
        compiler_params=_params(2), name="proj_" + epilogue.__name__[5:])(h16, w_t, *[a for a, _ in extra])


def _group_norm_gate(o, gain, gate):
    mu = jnp.mean(o, axis=-1, keepdims=True)
    var = jnp.mean(jnp.square(o - mu), axis=-1, keepdims=True)
    return (o - mu) * lax.rsqrt(var + EPS) * gain * _silu(gate)


def _ret_prompt_body(q_ref, k_ref, v_ref, g_ref, intra_ref, crossd_ref, kdec_ref, sdec_ref, gn_ref, o_ref, st_ref):
    @pl.when(pl.program_id(1) == 0)
    def _():
        st_ref[...] = jnp.zeros(st_ref.shape, f32)

    for h, hs in enumerate(_heads(q_ref.shape[1])):
        q, k, v = q_ref[:, hs], k_ref[:, hs], v_ref[:, hs]
        state = st_ref[h]
        scores = lax.dot_general(q, k, _NT, preferred_element_type=f32) * intra_ref[h]
        o = jnp.dot(scores.astype(bf16), v, preferred_element_type=f32)
        o = o + jnp.dot(q, state.astype(bf16), preferred_element_type=f32) * crossd_ref[h]
        k_dec = (k.astype(f32) * kdec_ref[h]).astype(bf16)
        st_ref[h] = state * sdec_ref[h] + lax.dot_general(k_dec, v, _TN, preferred_element_type=f32)
        o_ref[:, hs] = _group_norm_gate(o, gn_ref[:, hs], g_ref[:, hs].astype(f32)).astype(o_ref.dtype)


def _ret_tables():
    lg = jnp.log1p(-jnp.exp2(-5.0 - jnp.arange(N_HEADS, dtype=f32)))[:, None, None]
    t = jnp.arange(RET_CHUNK, dtype=f32)
    diff = t[None, :, None] - t[None, None, :]
    intra = jnp.where(diff >= 0, jnp.exp(lg * jnp.maximum(diff, 0.0)), 0.0)
    ones = jnp.ones((1, 1, HEAD_DIM), f32)
    crossd = jnp.exp(lg * (t[None, :, None] + 1.0)) * ones
    kdec = jnp.exp(lg * (RET_CHUNK - 1.0 - t[None, :, None])) * ones
    sdec = jnp.exp(lg * RET_CHUNK) * ones
    step_dec = jnp.exp(lg) * ones
    return intra, crossd, kdec, sdec, step_dec


def _ret_prompt(qk16, vg16, ret_norm_g, batch, tables):
    n = qk16.shape[0]
    nc = n // batch // RET_CHUNK
    intra, crossd, kdec, sdec, _ = tables
    row = lambda b, c: b * nc + c
    full3 = lambda a: pl.BlockSpec(a.shape, lambda b, c: (0, 0, 0))
    blk = lambda col: pl.BlockSpec((RET_CHUNK, GROUP_WIDTH), lambda b, c: (row(b, c), col))
    return pl.pallas_call(
        _ret_prompt_body, grid=(batch, nc),
        in_specs=[blk(0), blk(1), blk(0), blk(1), full3(intra), full3(crossd), full3(kdec), full3(sdec),
                  pl.BlockSpec((1, GROUP_WIDTH), lambda b, c: (0, 0))],
        out_specs=[blk(0), pl.BlockSpec((None, N_HEADS, HEAD_DIM, HEAD_DIM), lambda b, c: (b, 0, 0, 0))],
        out_shape=[jax.ShapeDtypeStruct((n, GROUP_WIDTH), bf16),
                   jax.ShapeDtypeStruct((batch, N_HEADS, HEAD_DIM, HEAD_DIM), f32)],
        compiler_params=_params(2), name="ret_prompt")(
            qk16, qk16, vg16, vg16, intra, crossd, kdec, sdec, ret_norm_g.reshape(1, GROUP_WIDTH))


def _block_diag(x):
    xt = jnp.tile(x, (1, N_HEADS))
    lane = lax.broadcasted_iota(jnp.int32, xt.shape, 1)
    row = lax.broadcasted_iota(jnp.int32, xt.shape, 0)
    return jnp.where((lane >> 7) == row, xt, 0.0)


RET_SAMPLE_SEQS = 4


def _ret_sample_body(q_ref, k_ref, v_ref, g_ref, st_ref, dec3_ref, dec2_ref, gn_ref, o_ref, sn_ref):
    for i in range(q_ref.shape[0]):
        q16, k16, v16 = q_ref[i].astype(bf16), k_ref[i].astype(bf16), v_ref[i].astype(bf16)
        q, k, v = q16.astype(f32), k16.astype(f32), v16.astype(f32)
        state = st_ref[i]
        cross = jnp.dot(_block_diag(q).astype(bf16), state.reshape(GROUP_WIDTH, HEAD_DIM).astype(bf16),
                        preferred_element_type=f32) * dec2_ref[...]
        qk = jnp.sum(q * k, axis=-1, keepdims=True).astype(bf16).astype(f32)
        o = qk * v + cross
        upd = lax.dot_general(_block_diag(k).astype(bf16), v16, _TN, preferred_element_type=f32)
        sn_ref[i] = state * dec3_ref[...] + upd.reshape(state.shape)
        o_ref[i] = _group_norm_gate(o, gn_ref[...], g_ref[i]).astype(o_ref.dtype)


def _ret_sample(q3, k3, v3, g3, state_ret, layer, ret_norm_g, step_dec):
    bd = q3.shape[0]
    nb = RET_SAMPLE_SEQS if bd % RET_SAMPLE_SEQS == 0 else 1
    tok = pl.BlockSpec((nb, N_HEADS, HEAD_DIM), lambda b: (b, 0, 0))
    return pl.pallas_call(
        _ret_sample_body, grid=(bd // nb,),
        in_specs=[tok, tok, tok, tok,
                  pl.BlockSpec((None, nb, N_HEADS, HEAD_DIM, HEAD_DIM), lambda b: (layer, b, 0, 0, 0)),
                  pl.BlockSpec(step_dec.shape, lambda b: (0, 0, 0)),
                  pl.BlockSpec((N_HEADS, HEAD_DIM), lambda b: (0, 0)),
                  pl.BlockSpec((N_HEADS, HEAD_DIM), lambda b: (0, 0))],
        out_specs=[tok, pl.BlockSpec((nb, N_HEADS, HEAD_DIM, HEAD_DIM), lambda b: (b, 0, 0, 0))],
        out_shape=[jax.ShapeDtypeStruct((bd, N_HEADS, HEAD_DIM), f32),
                   jax.ShapeDtypeStruct((bd, N_HEADS, HEAD_DIM, HEAD_DIM), f32)],
        compiler_params=_params(1), name="ret_sample")(
            q3, k3, v3, g3, state_ret, step_dec, step_dec.reshape(N_HEADS, HEAD_DIM),
            ret_norm_g.reshape(N_HEADS, HEAD_DIM))


def _cumsum_body(x_ref, o_ref, carry_ref):
    @pl.when(pl.program_id(1) == 0)
    def _():
        carry_ref[...] = jnp.zeros(carry_ref.shape, f32)

    tb = x_ref.shape[1]
    upper = (lax.broadcasted_iota(jnp.int32, (tb, tb), 0) <= lax.broadcasted_iota(jnp.int32, (tb, tb), 1)).astype(f32)
    cum = jnp.dot(x_ref[...], upper, precision=lax.Precision.HIGHEST, preferred_element_type=f32) + carry_ref[:, :1]
    o_ref[...] = cum
    carry_ref[...] = jnp.broadcast_to(cum[:, tb - 1:tb], carry_ref.shape)


def _cumsum_lanes(x_t, tb):
    b, h, s = x_t.shape
    spec = pl.BlockSpec((None, h, tb), lambda i, j: (i, 0, j))
    return pl.pallas_call(_cumsum_body, grid=(b, s // tb), in_specs=[spec], out_specs=spec,
                          out_shape=jax.ShapeDtypeStruct(x_t.shape, f32),
                          scratch_shapes=[pltpu.VMEM((h, LANES), f32)],
                          compiler_params=_params(2), name="forget_cumsum")(x_t)


LOG2E = 1.4426950408889634
HEADS_PER_STEP = 2


def _forget_tail(f_sel, first_lane, sign, ones_lane):
    x = f_sel * (sign * LOG2E)
    hi = x.astype(bf16).astype(f32)
    rest = x - hi
    mid = rest.astype(bf16).astype(f32)
    lo = (rest - mid).astype(bf16).astype(f32)
    lane = lax.broadcasted_iota(jnp.int32, (f_sel.shape[0], HEAD_DIM), 1)
    ones = jnp.where((lane >= ones_lane) & (lane < ones_lane + 3), 1.0, 0.0)
    tail = jnp.where(lane == first_lane, hi, jnp.where(lane == first_lane + 1, mid,
                                                      jnp.where(lane == first_lane + 2, lo, ones)))
    return tail.astype(bf16)


def _select_head(f_block, head):
    lane = lax.broadcasted_iota(jnp.int32, f_block.shape, 1)
    return jnp.sum(jnp.where(lane == head, f_block, 0.0), axis=-1, keepdims=True)


def _fox_prompt_body(q_ref, k_ref, v_ref, fq_ref, fk_ref, o_ref, qaug_ref, kaug_ref, m_ref, l_ref, acc_ref):
    pair, qi = pl.program_id(1), pl.program_id(2)
    tq = q_ref.shape[0]
    n_sub = q_ref.shape[1] // HEAD_DIM
    seq = k_ref.shape[0]

    @pl.when(qi == 0)
    def _():
        def fill(c, carry):
            rows = pl.ds(pl.multiple_of(c * tq, tq), tq)
            for hh, hs in enumerate(_heads(q_ref.shape[1])):
                kaug_ref[hh, rows, :HEAD_DIM] = k_ref[rows, hs]
                kaug_ref[hh, rows, HEAD_DIM:] = _forget_tail(_select_head(fk_ref[rows, :], pair * n_sub + hh), 3, -1.0, 0)
            return carry
        lax.fori_loop(0, seq // tq, fill, 0)

    for hh, hs in enumerate(_heads(q_ref.shape[1])):
        qaug_ref[hh, :, :HEAD_DIM] = q_ref[:, hs]
        qaug_ref[hh, :, HEAD_DIM:] = _forget_tail(_select_head(fq_ref[...], pair * n_sub + hh), 0, 1.0, 3)
    m_ref[...] = jnp.full(m_ref.shape, NEG_BIG, f32)
    l_ref[...] = jnp.zeros(l_ref.shape, f32)
    acc_ref[...] = jnp.zeros(acc_ref.shape, f32)

    def step(j, diagonal):
        rows = pl.ds(pl.multiple_of(j * tq, tq), tq)
        for hh, hs in enumerate(_heads(q_ref.shape[1])):
            s = lax.dot_general(qaug_ref[hh], kaug_ref[hh, rows, :], _NT, preferred_element_type=f32)
            if diagonal:
                r = lax.broadcasted_iota(jnp.int32, s.shape, 0)
                c = lax.broadcasted_iota(jnp.int32, s.shape, 1)
                s = jnp.where(c <= r, s, -jnp.inf)
            m_old = m_ref[hh]
            m_new = jnp.maximum(m_old, jnp.max(s, axis=-1, keepdims=True))
            alpha = jnp.exp2(m_old - m_new)
            p = jnp.exp2(s - jnp.tile(m_new, (1, tq // LANES)))
            l_ref[hh] = alpha * l_ref[hh] + jnp.sum(p, axis=-1, keepdims=True)
            acc_ref[hh] = alpha * acc_ref[hh] + jnp.dot(p.astype(bf16), v_ref[rows, hs], preferred_element_type=f32)
            m_ref[hh] = m_new

    def off_diagonal(j, carry):
        step(j, False)
        return carry

    lax.fori_loop(0, qi, off_diagonal, 0)
    step(qi, True)
    for hh, hs in enumerate(_heads(q_ref.shape[1])):
        o_ref[:, hs] = (acc_ref[hh] / l_ref[hh]).astype(o_ref.dtype)


def _fox_prompt(q16, k16, v16, f_cols, batch, tq):
    n = q16.shape[0]
    s = n // batch
    nq = s // tq
    width = HEADS_PER_STEP * HEAD_DIM
    kv = lambda a: a.reshape(batch, s, GROUP_WIDTH)
    kv_spec = pl.BlockSpec((None, s, width), lambda b, h, i: (b, 0, h))
    q_spec = pl.BlockSpec((tq, width), lambda b, h, i: (b * nq + i, h))
    sub = (HEADS_PER_STEP, tq, HEAD_DIM)
    return pl.pallas_call(
        _fox_prompt_body, grid=(batch, N_HEADS // HEADS_PER_STEP, nq),
        in_specs=[q_spec, kv_spec, kv_spec,
                  pl.BlockSpec((None, tq, N_HEADS), lambda b, h, i: (b, i, 0)),
                  pl.BlockSpec((None, s, N_HEADS), lambda b, h, i: (b, 0, 0))],
        out_specs=q_spec,
        out_shape=jax.ShapeDtypeStruct((n, GROUP_WIDTH), bf16),
        scratch_shapes=[pltpu.VMEM((HEADS_PER_STEP, tq, 2 * HEAD_DIM), bf16),
                        pltpu.VMEM((HEADS_PER_STEP, s, 2 * HEAD_DIM), bf16),
                        pltpu.VMEM(sub, f32), pltpu.VMEM(sub, f32), pltpu.VMEM(sub, f32)],
        compiler_params=_params(3), name="fox_prompt")(q16, kv(k16), kv(v16), f_cols, f_cols)


FLAT = PAGE_SIZE * N_HEADS
PAGES_PER_STEP = 8


def _dot_select(y, m01):
    hi = y.astype(bf16)
    rest = y - hi.astype(f32)
    mid = rest.astype(bf16)
    lo = (rest - mid.astype(f32)).astype(bf16)
    return sum(jnp.dot(piece, m01, preferred_element_type=f32) for piece in (hi, mid, lo))


def _fold_heads(z, own_head):
    z = jnp.where(own_head, z, 0.0)
    return jnp.sum(z.reshape(z.shape[0] // N_HEADS, N_HEADS, FLAT), axis=1)


def _forget_pages_body(x_ref, excl_ref, tot_ref):
    x = x_ref[...]
    t = x.shape[1]
    later = (lax.broadcasted_iota(jnp.int32, (t, t), 0) > lax.broadcasted_iota(jnp.int32, (t, t), 1)).astype(bf16)
    spread = (lax.broadcasted_iota(jnp.int32, (t, FLAT), 1) >> 3
              == lax.broadcasted_iota(jnp.int32, (t, FLAT), 0)).astype(bf16)
    excl = _dot_select(x, later)
    own_head = ((lax.broadcasted_iota(jnp.int32, (x.shape[0], FLAT), 1) & (N_HEADS - 1))
                == (lax.broadcasted_iota(jnp.int32, (x.shape[0], FLAT), 0) & (N_HEADS - 1)))
    excl_ref[...] = _fold_heads(_dot_select(excl, spread), own_head)
    tot_ref[...] = _fold_heads(jnp.broadcast_to(jnp.sum(x, axis=-1, keepdims=True), own_head.shape), own_head)


def _forget_pages(cache_logf):
    depth, n_phys, page, heads = cache_logf.shape
    assert page == PAGE_SIZE and heads == N_HEADS
    n = depth * n_phys
    assert n % SUBLANES == 0
    pp = next((c for c in (64, 32, 16, 8) if n % c == 0), n)
    rows = cache_logf.transpose(0, 1, 3, 2).reshape(n * heads, page)
    out = jax.ShapeDtypeStruct((n, FLAT), f32)
    excl, tot = pl.pallas_call(
        _forget_pages_body, grid=(n // pp,),
        in_specs=[pl.BlockSpec((pp * heads, page), lambda i: (i, 0))],
        out_specs=[pl.BlockSpec((pp, FLAT), lambda i: (i, 0))] * 2, out_shape=[out, out],
        compiler_params=_params(1), name="forget_pages")(rows)
    return excl, tot


def _fox_sample_body(n_group, page_row0, pt_ref, q_ref, kn_ref, vn_ref, lnew_ref, *rest):
    k_refs, v_refs = rest[:n_group], rest[n_group:2 * n_group]
    excl_refs, tot_refs = rest[2 * n_group:3 * n_group], rest[3 * n_group:4 * n_group]
    o_ref, carry_ref, m_ref, l_ref, acc_ref = rest[4 * n_group:]
    step = pl.program_id(1)
    n_pages = pl.num_programs(1) * n_group
    scale = HEAD_DIM ** -0.5

    @pl.when(step == 0)
    def _():
        carry_ref[...] = lnew_ref[...]
        m_ref[...] = jnp.full(m_ref.shape, NEG_BIG, f32)
        l_ref[...] = jnp.zeros(l_ref.shape, f32)
        acc_ref[...] = jnp.zeros(acc_ref.shape, f32)

    q16 = q_ref[...].astype(bf16)
    own_head = ((lax.broadcasted_iota(jnp.int32, (N_HEADS, FLAT), 1) & (N_HEADS - 1))
                == lax.broadcasted_iota(jnp.int32, (N_HEADS, FLAT), 0))
    carry = carry_ref[...]
    scores = []
    for j in range(n_group):
        kf = k_refs[j][...].reshape(FLAT, HEAD_DIM).astype(bf16)
        row = pl.ds((page_row0 + pt_ref[pl.program_id(0), n_pages - 1 - (step * n_group + j)]) % SUBLANES, 1)
        s = lax.dot_general(q16, kf, _NT, preferred_element_type=f32) * scale + (carry + excl_refs[j][row, :])
        scores.append(jnp.where(own_head, s, -jnp.inf))
        carry = carry + tot_refs[j][row, :]
    carry_ref[...] = carry

    top = scores[0]
    for s in scores[1:]:
        top = jnp.maximum(top, s)
    m_old = m_ref[...]
    m_new = jnp.maximum(m_old, jnp.max(top, axis=-1, keepdims=True))
    alpha = jnp.exp(m_old - m_new)
    l_new = alpha * l_ref[...]
    acc = alpha * acc_ref[...]
    for j in range(n_group):
        pe = jnp.exp(scores[j] - m_new)
        l_new = l_new + jnp.sum(pe, axis=-1, keepdims=True)
        vf = v_refs[j][...].reshape(FLAT, HEAD_DIM).astype(bf16)
        acc = acc + jnp.dot(pe.astype(bf16), vf, preferred_element_type=f32)
    m_ref[...] = m_new
    l_ref[...] = l_new
    acc_ref[...] = acc

    @pl.when(step == pl.num_programs(1) - 1)
    def _():
        rounded = lambda ref: ref[...].astype(bf16).astype(f32)
        s_new = jnp.sum(rounded(q_ref) * rounded(kn_ref), axis=-1, keepdims=True) * scale
        m_fin = jnp.maximum(m_new, s_new)
        beta = jnp.exp(m_new - m_fin)
        p_new = jnp.exp(s_new - m_fin)
        num = beta * acc + p_new.astype(bf16).astype(f32) * rounded(vn_ref)
        o_ref[...] = (num / (beta * l_new + p_new)).astype(o_ref.dtype)


def _fox_sample(q3, kn3, vn3, logf_new, cache_k, cache_v, excl_flat, tot_flat, layer, page_table):
    bd, n_pages = page_table.shape
    n_phys = cache_k.shape[1]
    g = PAGES_PER_STEP if n_pages % PAGES_PER_STEP == 0 else 1
    lnew_flat = jnp.tile(logf_new, (1, PAGE_SIZE)).reshape(bd, 1, FLAT)
    tok = pl.BlockSpec((None, N_HEADS, HEAD_DIM), lambda b, s, pt: (b, 0, 0))

    def page_of(j):
        return lambda b, s, pt: pt[b, n_pages - 1 - (s * g + j)]

    def kv_spec(j):
        pg = page_of(j)
        return pl.BlockSpec((None, None, PAGE_SIZE, N_HEADS, HEAD_DIM), lambda b, s, pt: (layer, pg(b, s, pt), 0, 0, 0))

    def flat_spec(j):
        pg = page_of(j)
        return pl.BlockSpec((SUBLANES, FLAT), lambda b, s, pt: ((layer * n_phys + pg(b, s, pt)) // SUBLANES, 0))

    groups = [[kv_spec(j) for j in range(g)], [kv_spec(j) for j in range(g)],
              [flat_spec(j) for j in range(g)], [flat_spec(j) for j in range(g)]]
    grid_spec = pltpu.PrefetchScalarGridSpec(
        num_scalar_prefetch=1, grid=(bd, n_pages // g),
        in_specs=[tok, tok, tok, pl.BlockSpec((None, 1, FLAT), lambda b, s, pt: (b, 0, 0))] + sum(groups, []),
        out_specs=tok,
        scratch_shapes=[pltpu.VMEM((1, FLAT), f32), pltpu.VMEM((N_HEADS, 1), f32), pltpu.VMEM((N_HEADS, 1), f32),
                        pltpu.VMEM((N_HEADS, HEAD_DIM), f32)])
    return pl.pallas_call(
        functools.partial(_fox_sample_body, g, layer * n_phys), grid_spec=grid_spec,
        out_shape=jax.ShapeDtypeStruct((bd, N_HEADS, HEAD_DIM), f32),
        compiler_params=_params(2), name="fox_sample")(
            page_table, q3, kn3, vn3, lnew_flat, *([cache_k] * g + [cache_v] * g + [excl_flat] * g + [tot_flat] * g))


def _out_proj_body(ret_ref, fox_ref, w_ref, x_ref, ga_ref, o_ref, wb_ref):
    @pl.when(pl.program_id(1) == 0)
    def _():
        wb_ref[...] = w_ref[...].astype(bf16)

    half = ret_ref.shape[1]
    y = jnp.dot(ret_ref[...].astype(bf16), wb_ref[:half, :], preferred_element_type=f32)
    y = y + jnp.dot(fox_ref[...].astype(bf16), wb_ref[half:, :], preferred_element_type=f32)
    o_ref[...] = x_ref[...] + ga_ref[...] * y


def _out_proj(ret16, fox16, w_out, layer, x, mod, tm, tn):
    n, d = x.shape
    k = w_out.shape[1]
    ga, ga_spec = mod.operand(2, tm, tn, lambda j, i: i, lambda j, i: j)
    act = pl.BlockSpec((tm, ret16.shape[1]), lambda j, i: (i, 0))
    xs = pl.BlockSpec((tm, tn), lambda j, i: (i, j))
    return pl.pallas_call(
        _out_proj_body, grid=(d // tn, n // tm),
        in_specs=[act, act, pl.BlockSpec((None, k, tn), lambda j, i: (layer, 0, j)), xs, ga_spec],
        out_specs=xs, out_shape=jax.ShapeDtypeStruct((n, d), f32),
        scratch_shapes=[pltpu.VMEM((k, tn), bf16)],
        compiler_params=_params(2), name="out_proj")(ret16, fox16, w_out, x, ga)


def _expert(h, wg_ref, wu_ref, wd_ref):
    g = jnp.dot(h, wg_ref[...].astype(bf16), preferred_element_type=f32)
    u = jnp.dot(h, wu_ref[...].astype(bf16), preferred_element_type=f32)
    return jnp.dot((_silu(g) * u).astype(bf16), wd_ref[...].astype(bf16), preferred_element_type=f32)


def _moe_body(h_ref, comb_ref, wg_ref, wu_ref, wd_ref, x_ref, gm_ref, o_ref, acc_ref):
    e = pl.program_id(1)

    @pl.when(e == 0)
    def _():
        acc_ref[...] = jnp.zeros(acc_ref.shape, f32)

    y = _expert(h_ref[...], wg_ref, wu_ref, wd_ref)
    lane = lax.broadcasted_iota(jnp.int32, comb_ref.shape, 1)
    w = jnp.sum(jnp.where(lane == e, comb_ref[...], 0.0), axis=-1, keepdims=True)
    acc_ref[...] += w * y

    @pl.when(e == pl.num_programs(1) - 1)
    def _():
        o_ref[...] = x_ref[...] + gm_ref[...] * acc_ref[...]


def _moe(h16, comb, wg, wu, wd, layer, x, mod, tm):
    n, d = x.shape
    n_exp, _, de = wg.shape[1:]
    gm, gm_spec = mod.operand(5, tm, d, lambda i, e: i, lambda i, e: 0)
    rows = lambda w: pl.BlockSpec((tm, w), lambda i, e: (i, 0))
    return pl.pallas_call(
        _moe_body, grid=(n // tm, n_exp),
        in_specs=[rows(d), rows(LANES),
                  pl.BlockSpec((None, None, d, de), lambda i, e: (layer, e, 0, 0)),
                  pl.BlockSpec((None, None, d, de), lambda i, e: (layer, e, 0, 0)),
                  pl.BlockSpec((None, None, de, d), lambda i, e: (layer, e, 0, 0)),
                  rows(d), gm_spec],
        out_specs=rows(d), out_shape=jax.ShapeDtypeStruct((n, d), f32),
        scratch_shapes=[pltpu.VMEM((tm, d), f32)],
        compiler_params=_params(2), name="moe")(h16, comb, wg, wu, wd, x, gm)


N_GROUPS = N_EXPERTS // EXPERTS_PER_GROUP
ROW_DMA_UNROLL = 8


def _group_layout(group, tile):
    n = group.shape[0]
    onehot = (group[:, None] == jnp.arange(N_GROUPS, dtype=jnp.int32)[None, :]).astype(jnp.int32)
    running = jnp.cumsum(onehot, axis=0)
    rank = jnp.sum(onehot * running, axis=1) - 1
    padded = (running[-1] + tile - 1) // tile * tile
    ends = jnp.cumsum(padded)
    slot = jnp.sum(onehot * (ends - padded)[None, :], axis=1) + rank
    starts = jnp.arange(n // tile + N_GROUPS, dtype=jnp.int32) * tile
    tile_group = jnp.minimum(jnp.sum((starts[:, None] >= ends[None, :]).astype(jnp.int32), axis=1), N_GROUPS - 1)
    return slot.astype(jnp.int32), tile_group.astype(jnp.int32), (ends[-1] // tile).astype(jnp.int32).reshape(1)


def _row_copies(n_rows, copy_of):
    def start(c, carry):
        for u in range(ROW_DMA_UNROLL):
            copy_of(c * ROW_DMA_UNROLL + u).start(priority=u % 2)
        return carry

    def wait(c, carry):
        for u in range(ROW_DMA_UNROLL):
            copy_of(c * ROW_DMA_UNROLL + u).wait()
        return carry

    lax.fori_loop(0, n_rows // ROW_DMA_UNROLL, start, 0)
    lax.fori_loop(0, n_rows // ROW_DMA_UNROLL, wait, 0)


def _sort_rows_body(chunk, slot_ref, src_hbm, dst_hbm, source_ref, sem):
    step = pl.program_id(0)

    @pl.when(step == 0)
    def _():
        def clear(s, carry):
            source_ref[s] = 0
            return carry

        def place(t, carry):
            source_ref[slot_ref[t]] = t
            return carry

        lax.fori_loop(0, source_ref.shape[0], clear, 0)
        lax.fori_loop(0, slot_ref.shape[0], place, 0)

    base = step * chunk
    _row_copies(chunk, lambda k: pltpu.make_async_copy(
        src_hbm.at[pl.ds(source_ref[base + k], 1)], dst_hbm.at[pl.ds(base + k, 1)], sem))


def _sort_rows(rows, slot, n_slots, chunk):
    grid_spec = pltpu.PrefetchScalarGridSpec(
        num_scalar_prefetch=1, grid=(n_slots // chunk,),
        in_specs=[pl.BlockSpec(memory_space=pl.ANY)], out_specs=pl.BlockSpec(memory_space=pl.ANY),
        scratch_shapes=[pltpu.SMEM((n_slots,), jnp.int32), pltpu.SemaphoreType.DMA(())])
    return pl.pallas_call(functools.partial(_sort_rows_body, chunk), grid_spec=grid_spec,
                          out_shape=jax.ShapeDtypeStruct((n_slots, rows.shape[1]), rows.dtype),
                          compiler_params=_params(1), name="moe_sort")(slot, rows)


def _moe_group_body(tg_ref, used_ref, hs_ref, wg_ref, wu_ref, wd_ref, o_ref):
    tile, e4 = pl.program_id(0), pl.program_id(1)
    d = wg_ref.shape[0]

    @pl.when(e4 == 0)
    def _():
        o_ref[...] = jnp.zeros(o_ref.shape, f32)

    @pl.when(tile < used_ref[0])
    def _():
        y = _expert(hs_ref[:, :d].astype(bf16), wg_ref, wu_ref, wd_ref)
        e = tg_ref[tile] * EXPERTS_PER_GROUP + e4
        comb = hs_ref[:, d:]
        lane = lax.broadcasted_iota(jnp.int32, comb.shape, 1)
        o_ref[...] += jnp.sum(jnp.where(lane == e, comb, 0.0), axis=-1, keepdims=True) * y


def _moe_group(hs, tile_group, n_used, wg, wu, wd, layer, tile):
    n_slots = hs.shape[0]
    d, de = wg.shape[2:]

    def expert(i, e4, tg, used):
        last = used[0] - 1
        return jnp.where(i <= last, tg[i] * EXPERTS_PER_GROUP + e4, tg[last] * EXPERTS_PER_GROUP + EXPERTS_PER_GROUP - 1)

    grid_spec = pltpu.PrefetchScalarGridSpec(
        num_scalar_prefetch=2, grid=(n_slots // tile, EXPERTS_PER_GROUP),
        in_specs=[pl.BlockSpec((tile, hs.shape[1]), lambda i, e4, tg, used: (i, 0)),
                  pl.BlockSpec((None, None, d, de), lambda i, e4, tg, used: (layer, expert(i, e4, tg, used), 0, 0)),
                  pl.BlockSpec((None, None, d, de), lambda i, e4, tg, used: (layer, expert(i, e4, tg, used), 0, 0)),
                  pl.BlockSpec((None, None, de, d), lambda i, e4, tg, used: (layer, expert(i, e4, tg, used), 0, 0))],
        out_specs=pl.BlockSpec((tile, d), lambda i, e4, tg, used: (i, 0)))
    return pl.pallas_call(_moe_group_body, grid_spec=grid_spec,
                          out_shape=jax.ShapeDtypeStruct((n_slots, d), f32),
                          compiler_params=_params(2), name="moe_group")(tile_group, n_used, hs, wg, wu, wd)


def _moe_combine_body(slot_ref, x_ref, gm_ref, ys_hbm, o_ref, rows_ref, sem):
    tm = x_ref.shape[0]
    base = pl.program_id(0) * tm
    _row_copies(tm, lambda k: pltpu.make_async_copy(
        ys_hbm.at[pl.ds(slot_ref[base + k], 1)], rows_ref.at[pl.ds(k, 1)], sem))
    o_ref[...] = x_ref[...] + gm_ref[...] * rows_ref[...]


def _moe_combine(x, mod, ys, slot, tm):
    n, d = x.shape
    gm, gm_spec = mod.operand(5, tm, d, lambda i, s: i, lambda i, s: 0)
    rows = pl.BlockSpec((tm, d), lambda i, s: (i, 0))
    grid_spec = pltpu.PrefetchScalarGridSpec(
        num_scalar_prefetch=1, grid=(n // tm,),
        in_specs=[rows, gm_spec, pl.BlockSpec(memory_space=pl.ANY)], out_specs=rows,
        scratch_shapes=[pltpu.VMEM((tm, d), f32), pltpu.SemaphoreType.DMA(())])
    return pl.pallas_call(_moe_combine_body, grid_spec=grid_spec, out_shape=jax.ShapeDtypeStruct((n, d), f32),
                          compiler_params=_params(1), name="moe_combine")(slot, x, gm, ys)


def _rope_tables(positions):
    half = HEAD_DIM // 2
    inv_freq = ROPE_BASE ** (-jnp.arange(half, dtype=f32) / half)
    ang = positions.astype(f32)[:, None] * inv_freq[None, :]
    cos, sin = jnp.cos(ang), jnp.sin(ang)
    return jnp.concatenate([cos, cos], axis=-1), jnp.concatenate([-sin, sin], axis=-1)


def _mixer_in(x, mod, layer, tm, pos_tables, pos_tiles, p, lo, q_scale):
    h16 = _modulate(x, p["norm_mix_g"][layer], mod, 1, 0, tm)
    cos, sin = pos_tables
    pos_spec = pl.BlockSpec((tm, HEAD_DIM), lambda j, i: (i % pos_tiles, 0))
    w_in = p["w_in_t"]
    gw = GROUP_WIDTH
    (qk16,) = _proj(h16, w_in, layer, 0, 2, gw, tm, _epi_rope, [(cos, pos_spec), (sin, pos_spec)], [lo])
    (vg16,) = _proj(h16, w_in, layer, 2, 2, gw, tm, _epi_plain, [], [lo])
    row = lambda a: (a.reshape(1, HEAD_DIM), pl.BlockSpec((1, HEAD_DIM), lambda j, i: (0, 0)))
    const = lambda c: row(jnp.full((HEAD_DIM,), c, f32))
    (fq16,) = _proj(h16, w_in, layer, 4, 1, gw, tm, _epi_headnorm, [row(p["q_norm_g"][layer]), const(q_scale)], [lo])
    fk32, fk16 = _proj(h16, w_in, layer, 5, 1, gw, tm, _epi_headnorm, [row(p["k_norm_g"][layer]), const(1.0)], [f32, lo])
    fv32, fv16 = _proj(h16, w_in, layer, 6, 1, gw, tm, _epi_plain, [], [f32, lo])
    n_forget = p["b_forget"].shape[1]
    w_f = jnp.pad(w_in[layer, 7 * gw:, :], ((0, LANES - n_forget), (0, 0)))[None]
    b_f = jnp.pad(p["b_forget"][layer], (0, LANES - n_forget)).reshape(1, LANES)
    (logf,) = _proj(h16, w_f, 0, 0, 1, LANES, tm, _epi_forget,
                    [(b_f, pl.BlockSpec((1, LANES), lambda j, i: (0, 0)))], [f32], out_width=n_forget)
    return qk16, vg16, fq16, fk32, fk16, fv32, fv16, logf


def _channel(x, mod, layer, tm, p, grouped):
    router = (p["w_router_pad"], p["b_router_pad"])
    experts = (p["w_gate"], p["w_up"], p["w_down"])
    if not grouped:
        h16, comb = _modulate(x, p["norm_ffn_g"][layer], mod, 4, 3, tm, router=router)
        return _moe(h16, comb, *experts, layer, x, mod, tm)
    hx, comb = _modulate(x, p["norm_ffn_g"][layer], mod, 4, 3, tm, router=router, pack_weights=True)
    slot, tile_group, n_used = _group_layout(comb[:, N_EXPERTS].astype(jnp.int32), tm)
    hs = _sort_rows(hx, slot, x.shape[0] + N_GROUPS * tm, tm)
    ys = _moe_group(hs, tile_group, n_used, *experts, layer, tm)
    return _moe_combine(x, mod, ys, slot, tm)


def kernel(x_prompt, x_sample, cache_k, cache_v, cache_logf, state_ret, page_table, c_prompt, c_sample, w_ada, b_ada,
           norm_mix_g, norm_ffn_g, w_in, b_forget, q_norm_g, k_norm_g, ret_norm_g, w_out, w_router, b_router,
           w_gate, w_up, w_down):
    batch, seq, d = x_prompt.shape
    bd, t_new, _ = x_sample.shape
    assert t_new == 1, "one new token per sampled sequence"
    assert bd % 8 == 0 and d == 2 * GROUP_WIDTH
    depth = w_in.shape[0]
    n_pages = page_table.shape[1]
    past_len = n_pages * PAGE_SIZE
    n_p = batch * seq
    tm_p = min(512, seq)
    tq = min(512, seq)

    p = dict(norm_mix_g=norm_mix_g, norm_ffn_g=norm_ffn_g, w_in_t=w_in.transpose(0, 2, 1), b_forget=b_forget,
             q_norm_g=q_norm_g, k_norm_g=k_norm_g,
             w_router_pad=jnp.pad(w_router, ((0, 0), (0, LANES - w_router.shape[1]))),
             b_router_pad=jnp.pad(b_router, (0, LANES - b_router.shape[0])).reshape(1, LANES),
             w_gate=w_gate, w_up=w_up, w_down=w_down)

    r_rows = -(-(bd + batch) // 8) * 8
    c_all = jnp.concatenate([c_sample, c_prompt, jnp.zeros((r_rows - bd - batch, d), f32)], axis=0)
    mod_all = _adaln(c_all, w_ada, b_ada)

    excl_flat, tot_flat = _forget_pages(cache_logf)
    tables = _ret_tables()
    rope_p = _rope_tables(jnp.arange(seq))
    rope_s = _rope_tables(jnp.full((bd,), past_len))

    xp = x_prompt.reshape(n_p, d)
    xs = x_sample.reshape(bd, d)
    outs = [[] for _ in range(8)]
    tok3 = lambda a: a.reshape(bd, N_HEADS, HEAD_DIM)
    for l in range(depth):
        mod = _Mod(mod_all, l, bd, seq)
        qk16, vg16, fq16, fk32, fk16, fv32, fv16, logf = _mixer_in(xp, mod, l, tm_p, rope_p, seq // tm_p, p, bf16,
                                                                       LOG2E * HEAD_DIM ** -0.5)
        ret16, s_fin = _ret_prompt(qk16, vg16, ret_norm_g[l], batch, tables)
        logf_b = logf.reshape(batch, seq, -1)
        f_rows = _cumsum_lanes(logf_b.transpose(0, 2, 1), tq)
        fox16 = _fox_prompt(fq16, fk16, fv16, f_rows.transpose(0, 2, 1), batch, tq)
        xp = _out_proj(ret16, fox16, w_out, l, xp, mod, tm_p, GROUP_WIDTH)
        xp = _channel(xp, mod, l, tm_p, p, grouped=True)
        outs[0].append(fk32.reshape(batch, seq, N_HEADS, HEAD_DIM))
        outs[1].append(fv32.reshape(batch, seq, N_HEADS, HEAD_DIM))
        outs[2].append(logf_b)
        outs[3].append(s_fin)
        mod = _Mod(mod_all, l, bd, None)
        qk16, vg16, fq16, fk32, fk16, fv32, fv16, logf = _mixer_in(xs, mod, l, bd, rope_s, 1, p, f32, 1.0)
        ret3, s_new = _ret_sample(tok3(qk16[:, :GROUP_WIDTH]), tok3(qk16[:, GROUP_WIDTH:]), tok3(vg16[:, :GROUP_WIDTH]),
                                  tok3(vg16[:, GROUP_WIDTH:]), state_ret, l, ret_norm_g[l], tables[4])
        fox3 = _fox_sample(tok3(fq16), tok3(fk16), tok3(fv16), logf, cache_k, cache_v, excl_flat, tot_flat, l, page_table)
        xs = _out_proj(ret3.reshape(bd, GROUP_WIDTH), fox3.reshape(bd, GROUP_WIDTH), w_out, l, xs, mod, bd, GROUP_WIDTH)
        xs = _channel(xs, mod, l, bd, p, grouped=False)
        outs[4].append(fk32.reshape(bd, 1, N_HEADS, HEAD_DIM))
        outs[5].append(fv32.reshape(bd, 1, N_HEADS, HEAD_DIM))
        outs[6].append(logf.reshape(bd, 1, -1))
        outs[7].append(s_new)
    return (xp.reshape(batch, seq, d), xs.reshape(bd, 1, d)) + tuple(jnp.stack(o) for o in outs)
```

```python
import functools

import jax
import jax.numpy as jnp
from jax import lax
from jax.experimental import pallas as pl
from jax.experimental.pallas import tpu as pltpu

f32, bf16 = jnp.float32, jnp.bfloat16

HEAD_DIM = 128
N_HEADS = 8
GROUP_WIDTH = N_HEADS * HEAD_DIM
RET_CHUNK = 128
PAGE_SIZE = 128
ROPE_BASE = 10000.0
N_EXPERTS = 16
EXPERTS_PER_GROUP = 4
EPS = 1e-6
LANES = 128
SUBLANES = 8
V7X_VMEM_LIMIT = 56 * 1024 * 1024
NEG_BIG = -1e30

_NT = (((1,), (1,)), ((), ()))
_TN = (((0,), (0,)), ((), ()))


def _params(n_axes):
    return pltpu.CompilerParams(dimension_semantics=("arbitrary",) * n_axes, vmem_limit_bytes=V7X_VMEM_LIMIT)


def _silu(x):
    return x * jax.nn.sigmoid(x)


def _log_sigmoid(x):
    return jnp.minimum(x, 0.0) - jnp.log1p(jnp.exp(-jnp.abs(x)))


def _adaln_body(c_ref, w_ref, b_ref, o_ref):
    s = _silu(c_ref[...]).astype(bf16)
    o_ref[...] = jnp.dot(s, w_ref[...].astype(bf16), preferred_element_type=f32) + b_ref[...]


def _adaln(c_all, w_ada, b_ada):
    depth, d, m6 = w_ada.shape
    nmod, r, tn = m6 // d, c_all.shape[0], 512
    nj = d // tn
    return pl.pallas_call(
        _adaln_body, grid=(depth, nmod, nj),
        in_specs=[pl.BlockSpec((r, d), lambda l, c, j: (0, 0)),
                  pl.BlockSpec((None, d, tn), lambda l, c, j: (l, 0, c * nj + j)),
                  pl.BlockSpec((None, 1, tn), lambda l, c, j: (l, 0, c * nj + j))],
        out_specs=pl.BlockSpec((None, None, r, tn), lambda l, c, j: (l, c, 0, j)),
        out_shape=jax.ShapeDtypeStruct((depth, nmod, r, d), f32),
        compiler_params=_params(3), name="adaln")(c_all, w_ada, b_ada.reshape(depth, 1, m6))


class _Mod:
    def __init__(self, mod, layer, n_sample, rows_per_seq):
        self.mod, self.layer, self.n_sample, self.rows_per_seq = mod, layer, n_sample, rows_per_seq

    def operand(self, chunk, tm, tn, row_of, col_of):
        depth, nmod, r, d = self.mod.shape
        l, bd, rps = self.layer, self.n_sample, self.rows_per_seq
        if rps is None:
            assert tm == bd
            return self.mod, pl.BlockSpec((None, None, tm, tn), lambda *g: (l, chunk, 0, col_of(*g)))
        return (self.mod.reshape(depth, nmod, r, 1, d),
                pl.BlockSpec((None, None, None, 1, tn), lambda *g: (l, chunk, bd + (row_of(*g) * tm) // rps, 0, col_of(*g))))


def _modulated(x_ref, g_ref, sc_ref, sh_ref):
    x = x_ref[...]
    y = x * lax.rsqrt(jnp.mean(x * x, axis=-1, keepdims=True) + EPS) * g_ref[...]
    return y * (1.0 + sc_ref[...]) + sh_ref[...]


def _modulate_body(x_ref, g_ref, sc_ref, sh_ref, o_ref):
    o_ref[...] = _modulated(x_ref, g_ref, sc_ref, sh_ref).astype(o_ref.dtype)


def _route(logits, b_router):
    scores = jax.nn.sigmoid(logits)
    biased = scores + b_router
    lane_i = lax.broadcasted_iota(jnp.int32, logits.shape, 1)
    lane = lane_i.astype(f32)
    grp = lane_i >> 2
    neg = -jnp.inf

    def top2(v):
        t1 = jnp.max(v, axis=-1, keepdims=True)
        i1 = jnp.min(jnp.where(v == t1, lane, float(LANES)), axis=-1, keepdims=True)
        v2 = jnp.where(lane == i1, neg, v)
        t2 = jnp.max(v2, axis=-1, keepdims=True)
        i2 = jnp.min(jnp.where(v2 == t2, lane, float(LANES)), axis=-1, keepdims=True)
        return t1, i1, t2, i2

    best, best_g = None, None
    for g in range(N_EXPERTS // EXPERTS_PER_GROUP):
        t1, _, t2, _ = top2(jnp.where(grp == g, biased, neg))
        gs = t1 + t2
        if best is None:
            best, best_g = gs, jnp.zeros(gs.shape, jnp.int32)
        else:
            take = gs > best
            best_g = jnp.where(take, g, best_g)
            best = jnp.where(take, gs, best)
    _, i1, _, i2 = top2(jnp.where(grp == best_g, biased, neg))
    s1 = jnp.sum(jnp.where(lane == i1, scores, 0.0), axis=-1, keepdims=True)
    s2 = jnp.sum(jnp.where(lane == i2, scores, 0.0), axis=-1, keepdims=True)
    den = s1 + s2
    return jnp.where(lane == i1, s1 / den, 0.0) + jnp.where(lane == i2, s2 / den, 0.0), best_g


def _modulate_route_body(x_ref, g_ref, sc_ref, sh_ref, wr_ref, br_ref, o_ref, comb_ref):
    h = _modulated(x_ref, g_ref, sc_ref, sh_ref)
    d = h.shape[1]
    o_ref[:, :d] = h.astype(o_ref.dtype)
    logits = jnp.dot(h, wr_ref[...], precision=lax.Precision.HIGHEST, preferred_element_type=f32)
    comb, group = _route(logits, br_ref[...])
    lane = lax.broadcasted_iota(jnp.int32, comb.shape, 1)
    comb = jnp.where(lane == N_EXPERTS, group.astype(f32), comb)
    comb_ref[...] = comb
    if o_ref.shape[1] > d:
        o_ref[:, d:] = comb


def _modulate(x, norm_g, mod, sc_chunk, sh_chunk, tm, router=None, pack_weights=False):
    n, d = x.shape
    row, col = (lambda i: i), (lambda i: 0)
    sc, sc_spec = mod.operand(sc_chunk, tm, d, row, col)
    sh, sh_spec = mod.operand(sh_chunk, tm, d, row, col)
    in_specs = [pl.BlockSpec((tm, d), lambda i: (i, 0)), pl.BlockSpec((1, d), lambda i: (0, 0)), sc_spec, sh_spec]
    args = [x, norm_g.reshape(1, d), sc, sh]
    out_specs = pl.BlockSpec((tm, d), lambda i: (i, 0))
    out_shape = jax.ShapeDtypeStruct((n, d), bf16)
    body = _modulate_body
    if router is not None:
        w_router_pad, b_router_pad = router
        in_specs += [pl.BlockSpec((d, LANES), lambda i: (0, 0)), pl.BlockSpec((1, LANES), lambda i: (0, 0))]
        args += [w_router_pad, b_router_pad]
        width = d + LANES if pack_weights else d
        out_specs = [pl.BlockSpec((tm, width), lambda i: (i, 0)), pl.BlockSpec((tm, LANES), lambda i: (i, 0))]
        out_shape = [jax.ShapeDtypeStruct((n, width), f32 if pack_weights else bf16),
                     jax.ShapeDtypeStruct((n, LANES), f32)]
        body = _modulate_route_body
    return pl.pallas_call(body, grid=(n // tm,), in_specs=in_specs, out_specs=out_specs, out_shape=out_shape,
                          compiler_params=_params(1), name="modulate")(*args)


def _proj_body(epilogue, n_extra, h_ref, w_ref, *rest):
    extra, outs, wb_ref = rest[:n_extra], rest[n_extra:-1], rest[-1]

    @pl.when(pl.program_id(1) == 0)
    def _():
        wb_ref[...] = w_ref[...].T.astype(bf16)

    acc = jnp.dot(h_ref[...], wb_ref[...], preferred_element_type=f32)
    epilogue(acc, extra, outs)


def _heads(width):
    return [slice(h * HEAD_DIM, (h + 1) * HEAD_DIM) for h in range(width // HEAD_DIM)]


def _epi_rope(acc, extra, outs):
    cos, sin = extra[0][...], extra[1][...]
    kscale = jnp.where(pl.program_id(0) == 1, HEAD_DIM ** -0.5, 1.0).astype(f32)
    for hs in _heads(acc.shape[1]):
        a = acc[:, hs]
        outs[0][:, hs] = ((a * cos + pltpu.roll(a, HEAD_DIM // 2, 1) * sin) * kscale).astype(outs[0].dtype)


def _epi_headnorm(acc, extra, outs):
    g, post = extra[0][...], extra[1][...]
    for hs in _heads(acc.shape[1]):
        a = acc[:, hs]
        y = a * lax.rsqrt(jnp.mean(a * a, axis=-1, keepdims=True) + EPS) * g
        for o in outs[:-1]:
            o[:, hs] = y.astype(o.dtype)
        outs[-1][:, hs] = (y * post).astype(outs[-1].dtype)


def _epi_plain(acc, extra, outs):
    for o in outs:
        o[...] = acc.astype(o.dtype)


def _epi_forget(acc, extra, outs):
    y = _log_sigmoid(acc + extra[0][...])
    outs[0][...] = y[:, :outs[0].shape[1]]


def _proj(h16, w_t, layer, col0, ncol, tn, tm, epilogue, extra, out_dtypes, out_width=None):
    n, k = h16.shape
    out_width = tn if out_width is None else out_width
    in_specs = [pl.BlockSpec((tm, k), lambda j, i: (i, 0)),
                pl.BlockSpec((None, tn, k), lambda j, i: (layer, col0 + j, 0))] + [s for _, s in extra]
    out_specs = [pl.BlockSpec((tm, out_width), lambda j, i: (i, j)) for _ in out_dtypes]
    out_shape = [jax.ShapeDtypeStruct((n, ncol * out_width), dt) for dt in out_dtypes]
    return pl.pallas_call(
        functools.partial(_proj_body, epilogue, len(extra)), grid=(ncol, n // tm),
        in_specs=in_specs, out_specs=out_specs, out_shape=out_shape,
        scratch_shapes=[pltpu.VMEM((k, tn), bf16)],
        compiler_params=_params(2), name="proj_" + epilogue.__name__[5:])(h16, w_t, *[a for a, _ in extra])


def _group_norm_gate(o, gain, gate):
    mu = jnp.mean(o, axis=-1, keepdims=True)
    var = jnp.mean(jnp.square(o - mu), axis=-1, keepdims=True)
    return (o - mu) * lax.rsqrt(var + EPS) * gain * _silu(gate)


def _ret_prompt_body(q_ref, k_ref, v_ref, g_ref, intra_ref, crossd_ref, kdec_ref, sdec_ref, gn_ref, o_ref, st_ref):
    @pl.when(pl.program_id(1) == 0)
    def _():
        st_ref[...] = jnp.zeros(st_ref.shape, f32)

    for h, hs in enumerate(_heads(q_ref.shape[1])):
        q, k, v = q_ref[:, hs], k_ref[:, hs], v_ref[:, hs]
        state = st_ref[h]
        scores = lax.dot_general(q, k, _NT, preferred_element_type=f32) * intra_ref[h]
        o = jnp.dot(scores.astype(bf16), v, preferred_element_type=f32)
        o = o + jnp.dot(q, state.astype(bf16), preferred_element_type=f32) * crossd_ref[h]
        k_dec = (k.astype(f32) * kdec_ref[h]).astype(bf16)
        st_ref[h] = state * sdec_ref[h] + lax.dot_general(k_dec, v, _TN, preferred_element_type=f32)
        o_ref[:, hs] = _group_norm_gate(o, gn_ref[:, hs], g_ref[:, hs].astype(f32)).astype(o_ref.dtype)


def _ret_tables():
    lg = jnp.log1p(-jnp.exp2(-5.0 - jnp.arange(N_HEADS, dtype=f32)))[:, None, None]
    t = jnp.arange(RET_CHUNK, dtype=f32)
    diff = t[None, :, None] - t[None, None, :]
    intra = jnp.where(diff >= 0, jnp.exp(lg * jnp.maximum(diff, 0.0)), 0.0)
    ones = jnp.ones((1, 1, HEAD_DIM), f32)
    crossd = jnp.exp(lg * (t[None, :, None] + 1.0)) * ones
    kdec = jnp.exp(lg * (RET_CHUNK - 1.0 - t[None, :, None])) * ones
    sdec = jnp.exp(lg * RET_CHUNK) * ones
    step_dec = jnp.exp(lg) * ones
    return intra, crossd, kdec, sdec, step_dec


def _ret_prompt(qk16, vg16, ret_norm_g, batch, tables):
    n = qk16.shape[0]
    nc = n // batch // RET_CHUNK
    intra, crossd, kdec, sdec, _ = tables
    row = lambda b, c: b * nc + c
    full3 = lambda a: pl.BlockSpec(a.shape, lambda b, c: (0, 0, 0))
    blk = lambda col: pl.BlockSpec((RET_CHUNK, GROUP_WIDTH), lambda b, c: (row(b, c), col))
    return pl.pallas_call(
        _ret_prompt_body, grid=(batch, nc),
        in_specs=[blk(0), blk(1), blk(0), blk(1), full3(intra), full3(crossd), full3(kdec), full3(sdec),
                  pl.BlockSpec((1, GROUP_WIDTH), lambda b, c: (0, 0))],
        out_specs=[blk(0), pl.BlockSpec((None, N_HEADS, HEAD_DIM, HEAD_DIM), lambda b, c: (b, 0, 0, 0))],
        out_shape=[jax.ShapeDtypeStruct((n, GROUP_WIDTH), bf16),
                   jax.ShapeDtypeStruct((batch, N_HEADS, HEAD_DIM, HEAD_DIM), f32)],
        compiler_params=_params(2), name="ret_prompt")(
            qk16, qk16, vg16, vg16, intra, crossd, kdec, sdec, ret_norm_g.reshape(1, GROUP_WIDTH))


def _block_diag(x):
    xt = jnp.tile(x, (1, N_HEADS))
    lane = lax.broadcasted_iota(jnp.int32, xt.shape, 1)
    row = lax.broadcasted_iota(jnp.int32, xt.shape, 0)
    return jnp.where((lane >> 7) == row, xt, 0.0)


RET_SAMPLE_SEQS = 4


def _ret_sample_body(q_ref, k_ref, v_ref, g_ref, st_ref, dec3_ref, dec2_ref, gn_ref, o_ref, sn_ref):
    for i in range(q_ref.shape[0]):
        q16, k16, v16 = q_ref[i].astype(bf16), k_ref[i].astype(bf16), v_ref[i].astype(bf16)
        q, k, v = q16.astype(f32), k16.astype(f32), v16.astype(f32)
        state = st_ref[i]
        cross = jnp.dot(_block_diag(q).astype(bf16), state.reshape(GROUP_WIDTH, HEAD_DIM).astype(bf16),
                        preferred_element_type=f32) * dec2_ref[...]
        qk = jnp.sum(q * k, axis=-1, keepdims=True).astype(bf16).astype(f32)
        o = qk * v + cross
        upd = lax.dot_general(_block_diag(k).astype(bf16), v16, _TN, preferred_element_type=f32)
        sn_ref[i] = state * dec3_ref[...] + upd.reshape(state.shape)
        o_ref[i] = _group_norm_gate(o, gn_ref[...], g_ref[i]).astype(o_ref.dtype)


def _ret_sample(q3, k3, v3, g3, state_ret, layer, ret_norm_g, step_dec):
    bd = q3.shape[0]
    nb = RET_SAMPLE_SEQS if bd % RET_SAMPLE_SEQS == 0 else 1
    tok = pl.BlockSpec((nb, N_HEADS, HEAD_DIM), lambda b: (b, 0, 0))
    return pl.pallas_call(
        _ret_sample_body, grid=(bd // nb,),
        in_specs=[tok, tok, tok, tok,
                  pl.BlockSpec((None, nb, N_HEADS, HEAD_DIM, HEAD_DIM), lambda b: (layer, b, 0, 0, 0)),
                  pl.BlockSpec(step_dec.shape, lambda b: (0, 0, 0)),
                  pl.BlockSpec((N_HEADS, HEAD_DIM), lambda b: (0, 0)),
                  pl.BlockSpec((N_HEADS, HEAD_DIM), lambda b: (0, 0))],
        out_specs=[tok, pl.BlockSpec((nb, N_HEADS, HEAD_DIM, HEAD_DIM), lambda b: (b, 0, 0, 0))],
        out_shape=[jax.ShapeDtypeStruct((bd, N_HEADS, HEAD_DIM), f32),
                   jax.ShapeDtypeStruct((bd, N_HEADS, HEAD_DIM, HEAD_DIM), f32)],
        compiler_params=_params(1), name="ret_sample")(
            q3, k3, v3, g3, state_ret, step_dec, step_dec.reshape(N_HEADS, HEAD_DIM),
            ret_norm_g.reshape(N_HEADS, HEAD_DIM))


def _cumsum_body(x_ref, o_ref, carry_ref):
    @pl.when(pl.program_id(1) == 0)
    def _():
        carry_ref[...] = jnp.zeros(carry_ref.shape, f32)

    tb = x_ref.shape[1]
    upper = (lax.broadcasted_iota(jnp.int32, (tb, tb), 0) <= lax.broadcasted_iota(jnp.int32, (tb, tb), 1)).astype(f32)
    cum = jnp.dot(x_ref[...], upper, precision=lax.Precision.HIGHEST, preferred_element_type=f32) + carry_ref[:, :1]
    o_ref[...] = cum
    carry_ref[...] = jnp.broadcast_to(cum[:, tb - 1:tb], carry_ref.shape)


def _cumsum_lanes(x_t, tb):
    b, h, s = x_t.shape
    spec = pl.BlockSpec((None, h, tb), lambda i, j: (i, 0, j))
    return pl.pallas_call(_cumsum_body, grid=(b, s // tb), in_specs=[spec], out_specs=spec,
                          out_shape=jax.ShapeDtypeStruct(x_t.shape, f32),
                          scratch_shapes=[pltpu.VMEM((h, LANES), f32)],
                          compiler_params=_params(2), name="forget_cumsum")(x_t)


LOG2E = 1.4426950408889634
HEADS_PER_STEP = 2


def _forget_tail(f_sel, first_lane, sign, ones_lane):
    x = f_sel * (sign * LOG2E)
    hi = x.astype(bf16).astype(f32)
    rest = x - hi
    mid = rest.astype(bf16).astype(f32)
    lo = (rest - mid).astype(bf16).astype(f32)
    lane = lax.broadcasted_iota(jnp.int32, (f_sel.shape[0], HEAD_DIM), 1)
    ones = jnp.where((lane >= ones_lane) & (lane < ones_lane + 3), 1.0, 0.0)
    tail = jnp.where(lane == first_lane, hi, jnp.where(lane == first_lane + 1, mid,
                                                      jnp.where(lane == first_lane + 2, lo, ones)))
    return tail.astype(bf16)


def _select_head(f_block, head):
    lane = lax.broadcasted_iota(jnp.int32, f_block.shape, 1)
    return jnp.sum(jnp.where(lane == head, f_block, 0.0), axis=-1, keepdims=True)


def _fox_prompt_body(q_ref, k_ref, v_ref, fq_ref, fk_ref, o_ref, qaug_ref, kaug_ref, m_ref, l_ref, acc_ref):
    pair, qi = pl.program_id(1), pl.program_id(2)
    tq = q_ref.shape[0]
    n_sub = q_ref.shape[1] // HEAD_DIM
    seq = k_ref.shape[0]

    @pl.when(qi == 0)
    def _():
        def fill(c, carry):
            rows = pl.ds(pl.multiple_of(c * tq, tq), tq)
            for hh, hs in enumerate(_heads(q_ref.shape[1])):
                kaug_ref[hh, rows, :HEAD_DIM] = k_ref[rows, hs]
                kaug_ref[hh, rows, HEAD_DIM:] = _forget_tail(_select_head(fk_ref[rows, :], pair * n_sub + hh), 3, -1.0, 0)
            return carry
        lax.fori_loop(0, seq // tq, fill, 0)

    for hh, hs in enumerate(_heads(q_ref.shape[1])):
        qaug_ref[hh, :, :HEAD_DIM] = q_ref[:, hs]
        qaug_ref[hh, :, HEAD_DIM:] = _forget_tail(_select_head(fq_ref[...], pair * n_sub + hh), 0, 1.0, 3)
    m_ref[...] = jnp.full(m_ref.shape, NEG_BIG, f32)
    l_ref[...] = jnp.zeros(l_ref.shape, f32)
    acc_ref[...] = jnp.zeros(acc_ref.shape, f32)

    def step(j, diagonal):
        rows = pl.ds(pl.multiple_of(j * tq, tq), tq)
        for hh, hs in enumerate(_heads(q_ref.shape[1])):
            s = lax.dot_general(qaug_ref[hh], kaug_ref[hh, rows, :], _NT, preferred_element_type=f32)
            if diagonal:
                r = lax.broadcasted_iota(jnp.int32, s.shape, 0)
                c = lax.broadcasted_iota(jnp.int32, s.shape, 1)
                s = jnp.where(c <= r, s, -jnp.inf)
            m_old = m_ref[hh]
            m_new = jnp.maximum(m_old, jnp.max(s, axis=-1, keepdims=True))
            alpha = jnp.exp2(m_old - m_new)
            p = jnp.exp2(s - jnp.tile(m_new, (1, tq // LANES)))
            l_ref[hh] = alpha * l_ref[hh] + jnp.sum(p, axis=-1, keepdims=True)
            acc_ref[hh] = alpha * acc_ref[hh] + jnp.dot(p.astype(bf16), v_ref[rows, hs], preferred_element_type=f32)
            m_ref[hh] = m_new

    def off_diagonal(j, carry):
        step(j, False)
        return carry

    lax.fori_loop(0, qi, off_diagonal, 0)
    step(qi, True)
    for hh, hs in enumerate(_heads(q_ref.shape[1])):
        o_ref[:, hs] = (acc_ref[hh] / l_ref[hh]).astype(o_ref.dtype)


def _fox_prompt(q16, k16, v16, f_cols, batch, tq):
    n = q16.shape[0]
    s = n // batch
    nq = s // tq
    width = HEADS_PER_STEP * HEAD_DIM
    kv = lambda a: a.reshape(batch, s, GROUP_WIDTH)
    kv_spec = pl.BlockSpec((None, s, width), lambda b, h, i: (b, 0, h))
    q_spec = pl.BlockSpec((tq, width), lambda b, h, i: (b * nq + i, h))
    sub = (HEADS_PER_STEP, tq, HEAD_DIM)
    return pl.pallas_call(
        _fox_prompt_body, grid=(batch, N_HEADS // HEADS_PER_STEP, nq),
        in_specs=[q_spec, kv_spec, kv_spec,
                  pl.BlockSpec((None, tq, N_HEADS), lambda b, h, i: (b, i, 0)),
                  pl.BlockSpec((None, s, N_HEADS), lambda b, h, i: (b, 0, 0))],
        out_specs=q_spec,
        out_shape=jax.ShapeDtypeStruct((n, GROUP_WIDTH), bf16),
        scratch_shapes=[pltpu.VMEM((HEADS_PER_STEP, tq, 2 * HEAD_DIM), bf16),
                        pltpu.VMEM((HEADS_PER_STEP, s, 2 * HEAD_DIM), bf16),
                        pltpu.VMEM(sub, f32), pltpu.VMEM(sub, f32), pltpu.VMEM(sub, f32)],
        compiler_params=_params(3), name="fox_prompt")(q16, kv(k16), kv(v16), f_cols, f_cols)


FLAT = PAGE_SIZE * N_HEADS
PAGES_PER_STEP = 8


def _dot_select(y, m01):
    hi = y.astype(bf16)
    rest = y - hi.astype(f32)
    mid = rest.astype(bf16)
    lo = (rest - mid.astype(f32)).astype(bf16)
    return sum(jnp.dot(piece, m01, preferred_element_type=f32) for piece in (hi, mid, lo))


def _forget_pages_body(x_ref, excl_ref, tot_ref):
    x = x_ref[...]
    t = x.shape[1]
    later = (lax.broadcasted_iota(jnp.int32, (t, t), 0) > lax.broadcasted_iota(jnp.int32, (t, t), 1)).astype(bf16)
    excl_ref[...] = _dot_select(x, later)
    tot_ref[...] = jnp.broadcast_to(jnp.sum(x, axis=-1, keepdims=True), x.shape)


def _forget_pages(cache_logf):
    depth, n_phys, page, heads = cache_logf.shape
    assert page == PAGE_SIZE and heads == N_HEADS
    n = depth * n_phys
    assert n % SUBLANES == 0
    nr = n * heads
    tr = next((c for c in (2048, 1024, 512, 256, 128, 64) if nr % c == 0), nr)
    rows = cache_logf.transpose(0, 1, 3, 2).reshape(nr, page)
    spec = pl.BlockSpec((tr, page), lambda i: (i, 0))
    out = jax.ShapeDtypeStruct((nr, page), f32)
    excl, tot = pl.pallas_call(_forget_pages_body, grid=(nr // tr,), in_specs=[spec], out_specs=[spec, spec],
                               out_shape=[out, out], compiler_params=_params(1), name="forget_pages")(rows)
    excl_flat = excl.reshape(n, heads, page).transpose(0, 2, 1).reshape(n, FLAT)
    tot_flat = jnp.tile(tot[:, 0].reshape(n, heads), (1, page))
    return excl_flat, tot_flat


def _fox_sample_body(n_group, page_row0, pt_ref, q_ref, kn_ref, vn_ref, lnew_ref, *rest):
    k_refs, v_refs = rest[:n_group], rest[n_group:2 * n_group]
    excl_refs, tot_refs = rest[2 * n_group:3 * n_group], rest[3 * n_group:4 * n_group]
    o_ref, carry_ref, m_ref, l_ref, acc_ref = rest[4 * n_group:]
    step = pl.program_id(1)
    n_pages = pl.num_programs(1) * n_group
    scale = HEAD_DIM ** -0.5

    @pl.when(step == 0)
    def _():
        carry_ref[...] = lnew_ref[...]
        m_ref[...] = jnp.full(m_ref.shape, NEG_BIG, f32)
        l_ref[...] = jnp.zeros(l_ref.shape, f32)
        acc_ref[...] = jnp.zeros(acc_ref.shape, f32)

    q16 = q_ref[...].astype(bf16)
    own_head = ((lax.broadcasted_iota(jnp.int32, (N_HEADS, FLAT), 1) & (N_HEADS - 1))
                == lax.broadcasted_iota(jnp.int32, (N_HEADS, FLAT), 0))
    carry = carry_ref[...]
    scores = []
    for j in range(n_group):
        kf = k_refs[j][...].reshape(FLAT, HEAD_DIM).astype(bf16)
        row = pl.ds((page_row0 + pt_ref[pl.program_id(0), n_pages - 1 - (step * n_group + j)]) % SUBLANES, 1)
        s = lax.dot_general(q16, kf, _NT, preferred_element_type=f32) * scale + (carry + excl_refs[j][row, :])
        scores.append(jnp.where(own_head, s, -jnp.inf))
        carry = carry + tot_refs[j][row, :]
    carry_ref[...] = carry

    top = scores[0]
    for s in scores[1:]:
        top = jnp.maximum(top, s)
    m_old = m_ref[...]
    m_new = jnp.maximum(m_old, jnp.max(top, axis=-1, keepdims=True))
    alpha = jnp.exp(m_old - m_new)
    l_new = alpha * l_ref[...]
    acc = alpha * acc_ref[...]
    for j in range(n_group):
        pe = jnp.exp(scores[j] - m_new)
        l_new = l_new + jnp.sum(pe, axis=-1, keepdims=True)
        vf = v_refs[j][...].reshape(FLAT, HEAD_DIM).astype(bf16)
        acc = acc + jnp.dot(pe.astype(bf16), vf, preferred_element_type=f32)
    m_ref[...] = m_new
    l_ref[...] = l_new
    acc_ref[...] = acc

    @pl.when(step == pl.num_programs(1) - 1)
    def _():
        rounded = lambda ref: ref[...].astype(bf16).astype(f32)
        s_new = jnp.sum(rounded(q_ref) * rounded(kn_ref), axis=-1, keepdims=True) * scale
        m_fin = jnp.maximum(m_new, s_new)
        beta = jnp.exp(m_new - m_fin)
        p_new = jnp.exp(s_new - m_fin)
        num = beta * acc + p_new.astype(bf16).astype(f32) * rounded(vn_ref)
        o_ref[...] = (num / (beta * l_new + p_new)).astype(o_ref.dtype)


def _fox_sample(q3, kn3, vn3, logf_new, cache_k, cache_v, excl_flat, tot_flat, layer, page_table):
    bd, n_pages = page_table.shape
    n_phys = cache_k.shape[1]
    g = PAGES_PER_STEP if n_pages % PAGES_PER_STEP == 0 else 1
    lnew_flat = jnp.tile(logf_new, (1, PAGE_SIZE)).reshape(bd, 1, FLAT)
    tok = pl.BlockSpec((None, N_HEADS, HEAD_DIM), lambda b, s, pt: (b, 0, 0))

    def page_of(j):
        return lambda b, s, pt: pt[b, n_pages - 1 - (s * g + j)]

    def kv_spec(j):
        pg = page_of(j)
        return pl.BlockSpec((None, None, PAGE_SIZE, N_HEADS, HEAD_DIM), lambda b, s, pt: (layer, pg(b, s, pt), 0, 0, 0))

    def flat_spec(j):
        pg = page_of(j)
        return pl.BlockSpec((SUBLANES, FLAT), lambda b, s, pt: ((layer * n_phys + pg(b, s, pt)) // SUBLANES, 0))

    groups = [[kv_spec(j) for j in range(g)], [kv_spec(j) for j in range(g)],
              [flat_spec(j) for j in range(g)], [flat_spec(j) for j in range(g)]]
    grid_spec = pltpu.PrefetchScalarGridSpec(
        num_scalar_prefetch=1, grid=(bd, n_pages // g),
        in_specs=[tok, tok, tok, pl.BlockSpec((None, 1, FLAT), lambda b, s, pt: (b, 0, 0))] + sum(groups, []),
        out_specs=tok,
        scratch_shapes=[pltpu.VMEM((1, FLAT), f32), pltpu.VMEM((N_HEADS, 1), f32), pltpu.VMEM((N_HEADS, 1), f32),
                        pltpu.VMEM((N_HEADS, HEAD_DIM), f32)])
    return pl.pallas_call(
        functools.partial(_fox_sample_body, g, layer * n_phys), grid_spec=grid_spec,
        out_shape=jax.ShapeDtypeStruct((bd, N_HEADS, HEAD_DIM), f32),
        compiler_params=_params(2), name="fox_sample")(
            page_table, q3, kn3, vn3, lnew_flat, *([cache_k] * g + [cache_v] * g + [excl_flat] * g + [tot_flat] * g))


def _out_proj_body(ret_ref, fox_ref, w_ref, x_ref, ga_ref, o_ref, wb_ref):
    @pl.when(pl.program_id(1) == 0)
    def _():
        wb_ref[...] = w_ref[...].astype(bf16)

    half = ret_ref.shape[1]
    y = jnp.dot(ret_ref[...].astype(bf16), wb_ref[:half, :], preferred_element_type=f32)
    y = y + jnp.dot(fox_ref[...].astype(bf16), wb_ref[half:, :], preferred_element_type=f32)
    o_ref[...] = x_ref[...] + ga_ref[...] * y


def _out_proj(ret16, fox16, w_out, layer, x, mod, tm, tn):
    n, d = x.shape
    k = w_out.shape[1]
    ga, ga_spec = mod.operand(2, tm, tn, lambda j, i: i, lambda j, i: j)
    act = pl.BlockSpec((tm, ret16.shape[1]), lambda j, i: (i, 0))
    xs = pl.BlockSpec((tm, tn), lambda j, i: (i, j))
    return pl.pallas_call(
        _out_proj_body, grid=(d // tn, n // tm),
        in_specs=[act, act, pl.BlockSpec((None, k, tn), lambda j, i: (layer, 0, j)), xs, ga_spec],
        out_specs=xs, out_shape=jax.ShapeDtypeStruct((n, d), f32),
        scratch_shapes=[pltpu.VMEM((k, tn), bf16)],
        compiler_params=_params(2), name="out_proj")(ret16, fox16, w_out, x, ga)


def _expert(h, wg_ref, wu_ref, wd_ref):
    g = jnp.dot(h, wg_ref[...].astype(bf16), preferred_element_type=f32)
    u = jnp.dot(h, wu_ref[...].astype(bf16), preferred_element_type=f32)
    return jnp.dot((_silu(g) * u).astype(bf16), wd_ref[...].astype(bf16), preferred_element_type=f32)


def _moe_body(h_ref, comb_ref, wg_ref, wu_ref, wd_ref, x_ref, gm_ref, o_ref, acc_ref):
    e = pl.program_id(1)

    @pl.when(e == 0)
    def _():
        acc_ref[...] = jnp.zeros(acc_ref.shape, f32)

    y = _expert(h_ref[...], wg_ref, wu_ref, wd_ref)
    lane = lax.broadcasted_iota(jnp.int32, comb_ref.shape, 1)
    w = jnp.sum(jnp.where(lane == e, comb_ref[...], 0.0), axis=-1, keepdims=True)
    acc_ref[...] += w * y

    @pl.when(e == pl.num_programs(1) - 1)
    def _():
        o_ref[...] = x_ref[...] + gm_ref[...] * acc_ref[...]


def _moe(h16, comb, wg, wu, wd, layer, x, mod, tm):
    n, d = x.shape
    n_exp, _, de = wg.shape[1:]
    gm, gm_spec = mod.operand(5, tm, d, lambda i, e: i, lambda i, e: 0)
    rows = lambda w: pl.BlockSpec((tm, w), lambda i, e: (i, 0))
    return pl.pallas_call(
        _moe_body, grid=(n // tm, n_exp),
        in_specs=[rows(d), rows(LANES),
                  pl.BlockSpec((None, None, d, de), lambda i, e: (layer, e, 0, 0)),
                  pl.BlockSpec((None, None, d, de), lambda i, e: (layer, e, 0, 0)),
                  pl.BlockSpec((None, None, de, d), lambda i, e: (layer, e, 0, 0)),
                  rows(d), gm_spec],
        out_specs=rows(d), out_shape=jax.ShapeDtypeStruct((n, d), f32),
        scratch_shapes=[pltpu.VMEM((tm, d), f32)],
        compiler_params=_params(2), name="moe")(h16, comb, wg, wu, wd, x, gm)


N_GROUPS = N_EXPERTS // EXPERTS_PER_GROUP
ROW_DMA_UNROLL = 8


def _group_layout(group, tile):
    n = group.shape[0]
    onehot = (group[:, None] == jnp.arange(N_GROUPS, dtype=jnp.int32)[None, :]).astype(jnp.int32)
    running = jnp.cumsum(onehot, axis=0)
    rank = jnp.sum(onehot * running, axis=1) - 1
    padded = (running[-1] + tile - 1) // tile * tile
    ends = jnp.cumsum(padded)
    slot = jnp.sum(onehot * (ends - padded)[None, :], axis=1) + rank
    starts = jnp.arange(n // tile + N_GROUPS, dtype=jnp.int32) * tile
    tile_group = jnp.minimum(jnp.sum((starts[:, None] >= ends[None, :]).astype(jnp.int32), axis=1), N_GROUPS - 1)
    return slot.astype(jnp.int32), tile_group.astype(jnp.int32), (ends[-1] // tile).astype(jnp.int32).reshape(1)


def _row_copies(n_rows, copy_of):
    def start(c, carry):
        for u in range(ROW_DMA_UNROLL):
            copy_of(c * ROW_DMA_UNROLL + u).start(priority=u % 2)
        return carry

    def wait(c, carry):
        for u in range(ROW_DMA_UNROLL):
            copy_of(c * ROW_DMA_UNROLL + u).wait()
        return carry

    lax.fori_loop(0, n_rows // ROW_DMA_UNROLL, start, 0)
    lax.fori_loop(0, n_rows // ROW_DMA_UNROLL, wait, 0)


def _sort_rows_body(chunk, slot_ref, src_hbm, o_ref, source_ref, sem):
    step = pl.program_id(0)

    @pl.when(step == 0)
    def _():
        def clear(s, carry):
            source_ref[s] = 0
            return carry

        def place(t, carry):
            source_ref[slot_ref[t]] = t
            return carry

        lax.fori_loop(0, source_ref.shape[0], clear, 0)
        lax.fori_loop(0, slot_ref.shape[0], place, 0)

    base = step * chunk
    _row_copies(chunk, lambda k: pltpu.make_async_copy(
        src_hbm.at[pl.ds(source_ref[base + k], 1)], o_ref.at[pl.ds(k, 1)], sem))


def _sort_rows(rows, slot, n_slots, chunk):
    grid_spec = pltpu.PrefetchScalarGridSpec(
        num_scalar_prefetch=1, grid=(n_slots // chunk,),
        in_specs=[pl.BlockSpec(memory_space=pl.ANY)],
        out_specs=pl.BlockSpec((chunk, rows.shape[1]), lambda i, s: (i, 0)),
        scratch_shapes=[pltpu.SMEM((n_slots,), jnp.int32), pltpu.SemaphoreType.DMA(())])
    return pl.pallas_call(functools.partial(_sort_rows_body, chunk), grid_spec=grid_spec,
                          out_shape=jax.ShapeDtypeStruct((n_slots, rows.shape[1]), rows.dtype),
                          compiler_params=_params(1), name="moe_sort")(slot, rows)


def _moe_group_body(tg_ref, used_ref, hs_ref, wg_ref, wu_ref, wd_ref, o_ref):
    tile, e4 = pl.program_id(0), pl.program_id(1)
    d = wg_ref.shape[0]

    @pl.when(e4 == 0)
    def _():
        o_ref[...] = jnp.zeros(o_ref.shape, f32)

    @pl.when(tile < used_ref[0])
    def _():
        y = _expert(hs_ref[:, :d].astype(bf16), wg_ref, wu_ref, wd_ref)
        e = tg_ref[tile] * EXPERTS_PER_GROUP + e4
        comb = hs_ref[:, d:]
        lane = lax.broadcasted_iota(jnp.int32, comb.shape, 1)
        o_ref[...] += jnp.sum(jnp.where(lane == e, comb, 0.0), axis=-1, keepdims=True) * y


def _moe_group(hs, tile_group, n_used, wg, wu, wd, layer, tile):
    n_slots = hs.shape[0]
    d, de = wg.shape[2:]

    def expert(i, e4, tg, used):
        last = used[0] - 1
        return jnp.where(i <= last, tg[i] * EXPERTS_PER_GROUP + e4, tg[last] * EXPERTS_PER_GROUP + EXPERTS_PER_GROUP - 1)

    grid_spec = pltpu.PrefetchScalarGridSpec(
        num_scalar_prefetch=2, grid=(n_slots // tile, EXPERTS_PER_GROUP),
        in_specs=[pl.BlockSpec((tile, hs.shape[1]), lambda i, e4, tg, used: (i, 0)),
                  pl.BlockSpec((None, None, d, de), lambda i, e4, tg, used: (layer, expert(i, e4, tg, used), 0, 0)),
                  pl.BlockSpec((None, None, d, de), lambda i, e4, tg, used: (layer, expert(i, e4, tg, used), 0, 0)),
                  pl.BlockSpec((None, None, de, d), lambda i, e4, tg, used: (layer, expert(i, e4, tg, used), 0, 0))],
        out_specs=pl.BlockSpec((tile, d), lambda i, e4, tg, used: (i, 0)))
    return pl.pallas_call(_moe_group_body, grid_spec=grid_spec,
                          out_shape=jax.ShapeDtypeStruct((n_slots, d), f32),
                          compiler_params=_params(2), name="moe_group")(tile_group, n_used, hs, wg, wu, wd)


def _moe_combine_body(slot_ref, x_ref, gm_ref, ys_hbm, o_ref, rows_ref, sem):
    tm = x_ref.shape[0]
    base = pl.program_id(0) * tm
    _row_copies(tm, lambda k: pltpu.make_async_copy(
        ys_hbm.at[pl.ds(slot_ref[base + k], 1)], rows_ref.at[pl.ds(k, 1)], sem))
    o_ref[...] = x_ref[...] + gm_ref[...] * rows_ref[...]


def _moe_combine(x, mod, ys, slot, tm):
    n, d = x.shape
    gm, gm_spec = mod.operand(5, tm, d, lambda i, s: i, lambda i, s: 0)
    rows = pl.BlockSpec((tm, d), lambda i, s: (i, 0))
    grid_spec = pltpu.PrefetchScalarGridSpec(
        num_scalar_prefetch=1, grid=(n // tm,),
        in_specs=[rows, gm_spec, pl.BlockSpec(memory_space=pl.ANY)], out_specs=rows,
        scratch_shapes=[pltpu.VMEM((tm, d), f32), pltpu.SemaphoreType.DMA(())])
    return pl.pallas_call(_moe_combine_body, grid_spec=grid_spec, out_shape=jax.ShapeDtypeStruct((n, d), f32),
                          compiler_params=_params(1), name="moe_combine")(slot, x, gm, ys)


def _rope_tables(positions):
    half = HEAD_DIM // 2
    inv_freq = ROPE_BASE ** (-jnp.arange(half, dtype=f32) / half)
    ang = positions.astype(f32)[:, None] * inv_freq[None, :]
    cos, sin = jnp.cos(ang), jnp.sin(ang)
    return jnp.concatenate([cos, cos], axis=-1), jnp.concatenate([-sin, sin], axis=-1)


def _mixer_in(x, mod, layer, tm, pos_tables, pos_tiles, p, lo, q_scale):
    h16 = _modulate(x, p["norm_mix_g"][layer], mod, 1, 0, tm)
    cos, sin = pos_tables
    pos_spec = pl.BlockSpec((tm, HEAD_DIM), lambda j, i: (i % pos_tiles, 0))
    w_in = p["w_in_t"]
    gw = GROUP_WIDTH
    (qk16,) = _proj(h16, w_in, layer, 0, 2, gw, tm, _epi_rope, [(cos, pos_spec), (sin, pos_spec)], [lo])
    (vg16,) = _proj(h16, w_in, layer, 2, 2, gw, tm, _epi_plain, [], [lo])
    row = lambda a: (a.reshape(1, HEAD_DIM), pl.BlockSpec((1, HEAD_DIM), lambda j, i: (0, 0)))
    const = lambda c: row(jnp.full((HEAD_DIM,), c, f32))
    (fq16,) = _proj(h16, w_in, layer, 4, 1, gw, tm, _epi_headnorm, [row(p["q_norm_g"][layer]), const(q_scale)], [lo])
    fk32, fk16 = _proj(h16, w_in, layer, 5, 1, gw, tm, _epi_headnorm, [row(p["k_norm_g"][layer]), const(1.0)], [f32, lo])
    fv32, fv16 = _proj(h16, w_in, layer, 6, 1, gw, tm, _epi_plain, [], [f32, lo])
    n_forget = p["b_forget"].shape[1]
    w_f = jnp.pad(w_in[layer, 7 * gw:, :], ((0, LANES - n_forget), (0, 0)))[None]
    b_f = jnp.pad(p["b_forget"][layer], (0, LANES - n_forget)).reshape(1, LANES)
    (logf,) = _proj(h16, w_f, 0, 0, 1, LANES, tm, _epi_forget,
                    [(b_f, pl.BlockSpec((1, LANES), lambda j, i: (0, 0)))], [f32], out_width=n_forget)
    return qk16, vg16, fq16, fk32, fk16, fv32, fv16, logf


def _channel(x, mod, layer, tm, p, grouped):
    router = (p["w_router_pad"], p["b_router_pad"])
    experts = (p["w_gate"], p["w_up"], p["w_down"])
    if not grouped:
        h16, comb = _modulate(x, p["norm_ffn_g"][layer], mod, 4, 3, tm, router=router)
        return _moe(h16, comb, *experts, layer, x, mod, tm)
    hx, comb = _modulate(x, p["norm_ffn_g"][layer], mod, 4, 3, tm, router=router, pack_weights=True)
    slot, tile_group, n_used = _group_layout(comb[:, N_EXPERTS].astype(jnp.int32), tm)
    hs = _sort_rows(hx, slot, x.shape[0] + N_GROUPS * tm, tm)
    ys = _moe_group(hs, tile_group, n_used, *experts, layer, tm)
    return _moe_combine(x, mod, ys, slot, tm)


def kernel(x_prompt, x_sample, cache_k, cache_v, cache_logf, state_ret, page_table, c_prompt, c_sample, w_ada, b_ada,
           norm_mix_g, norm_ffn_g, w_in, b_forget, q_norm_g, k_norm_g, ret_norm_g, w_out, w_router, b_router,
           w_gate, w_up, w_down):
    batch, seq, d = x_prompt.shape
    bd, t_new, _ = x_sample.shape
    assert t_new == 1, "one new token per sampled sequence"
    assert bd % 8 == 0 and d == 2 * GROUP_WIDTH
    depth = w_in.shape[0]
    n_pages = page_table.shape[1]
    past_len = n_pages * PAGE_SIZE
    n_p = batch * seq
    tm_p = min(512, seq)
    tq = min(512, seq)

    p = dict(norm_mix_g=norm_mix_g, norm_ffn_g=norm_ffn_g, w_in_t=w_in.transpose(0, 2, 1), b_forget=b_forget,
             q_norm_g=q_norm_g, k_norm_g=k_norm_g,
             w_router_pad=jnp.pad(w_router, ((0, 0), (0, LANES - w_router.shape[1]))),
             b_router_pad=jnp.pad(b_router, (0, LANES - b_router.shape[0])).reshape(1, LANES),
             w_gate=w_gate, w_up=w_up, w_down=w_down)

    r_rows = -(-(bd + batch) // 8) * 8
    c_all = jnp.concatenate([c_sample, c_prompt, jnp.zeros((r_rows - bd - batch, d), f32)], axis=0)
    mod_all = _adaln(c_all, w_ada, b_ada)

    excl_flat, tot_flat = _forget_pages(cache_logf)
    tables = _ret_tables()
    rope_p = _rope_tables(jnp.arange(seq))
    rope_s = _rope_tables(jnp.full((bd,), past_len))

    xp = x_prompt.reshape(n_p, d)
    xs = x_sample.reshape(bd, d)
    outs = [[] for _ in range(8)]
    tok3 = lambda a: a.reshape(bd, N_HEADS, HEAD_DIM)
    for l in range(depth):
        mod = _Mod(mod_all, l, bd, seq)
        qk16, vg16, fq16, fk32, fk16, fv32, fv16, logf = _mixer_in(xp, mod, l, tm_p, rope_p, seq // tm_p, p, bf16,
                                                                       LOG2E * HEAD_DIM ** -0.5)
        ret16, s_fin = _ret_prompt(qk16, vg16, ret_norm_g[l], batch, tables)
        logf_b = logf.reshape(batch, seq, -1)
        f_rows = _cumsum_lanes(logf_b.transpose(0, 2, 1), tq)
        fox16 = _fox_prompt(fq16, fk16, fv16, f_rows.transpose(0, 2, 1), batch, tq)
        xp = _out_proj(ret16, fox16, w_out, l, xp, mod, tm_p, GROUP_WIDTH)
        xp = _channel(xp, mod, l, tm_p, p, grouped=True)
        outs[0].append(fk32.reshape(batch, seq, N_HEADS, HEAD_DIM))
        outs[1].append(fv32.reshape(batch, seq, N_HEADS, HEAD_DIM))
        outs[2].append(logf_b)
        outs[3].append(s_fin)
        mod = _Mod(mod_all, l, bd, None)
        qk16, vg16, fq16, fk32, fk16, fv32, fv16, logf = _mixer_in(xs, mod, l, bd, rope_s, 1, p, f32, 1.0)
        ret3, s_new = _ret_sample(tok3(qk16[:, :GROUP_WIDTH]), tok3(qk16[:, GROUP_WIDTH:]), tok3(vg16[:, :GROUP_WIDTH]),
                                  tok3(vg16[:, GROUP_WIDTH:]), state_ret, l, ret_norm_g[l], tables[4])
        fox3 = _fox_sample(tok3(fq16), tok3(fk16), tok3(fv16), logf, cache_k, cache_v, excl_flat, tot_flat, l, page_table)
        xs = _out_proj(ret3.reshape(bd, GROUP_WIDTH), fox3.reshape(bd, GROUP_WIDTH), w_out, l, xs, mod, bd, GROUP_WIDTH)
        xs = _channel(xs, mod, l, bd, p, grouped=False)
        outs[4].append(fk32.reshape(bd, 1, N_HEADS, HEAD_DIM))
        outs[5].append(fv32.reshape(bd, 1, N_HEADS, HEAD_DIM))
        outs[6].append(logf.reshape(bd, 1, -1))
        outs[7].append(s_new)
    return (xp.reshape(batch, seq, d), xs.reshape(bd, 1, d)) + tuple(jnp.stack(o) for o in outs)
```

```python
import functools

import jax
import jax.numpy as jnp
from jax import lax
from jax.experimental import pallas as pl
from jax.experimental.pallas import tpu as pltpu

f32, bf16 = jnp.float32, jnp.bfloat16

HEAD_DIM = 128
N_HEADS = 8
GROUP_WIDTH = N_HEADS * HEAD_DIM
RET_CHUNK = 128
PAGE_SIZE = 128
ROPE_BASE = 10000.0
N_EXPERTS = 16
EXPERTS_PER_GROUP = 4
EPS = 1e-6
LANES = 128
SUBLANES = 8
V7X_VMEM_LIMIT = 56 * 1024 * 1024
NEG_BIG = -1e30

_NT = (((1,), (1,)), ((), ()))
_TN = (((0,), (0,)), ((), ()))


def _params(n_axes):
    return pltpu.CompilerParams(dimension_semantics=("arbitrary",) * n_axes, vmem_limit_bytes=V7X_VMEM_LIMIT)


def _silu(x):
    return x * jax.nn.sigmoid(x)


def _log_sigmoid(x):
    return jnp.minimum(x, 0.0) - jnp.log1p(jnp.exp(-jnp.abs(x)))


def _adaln_body(c_ref, w_ref, b_ref, o_ref):
    s = _silu(c_ref[...]).astype(bf16)
    o_ref[...] = jnp.dot(s, w_ref[...].astype(bf16), preferred_element_type=f32) + b_ref[...]


def _adaln(c_all, w_ada, b_ada):
    depth, d, m6 = w_ada.shape
    nmod, r, tn = m6 // d, c_all.shape[0], 512
    nj = d // tn
    return pl.pallas_call(
        _adaln_body, grid=(depth, nmod, nj),
        in_specs=[pl.BlockSpec((r, d), lambda l, c, j: (0, 0)),
                  pl.BlockSpec((None, d, tn), lambda l, c, j: (l, 0, c * nj + j)),
                  pl.BlockSpec((None, 1, tn), lambda l, c, j: (l, 0, c * nj + j))],
        out_specs=pl.BlockSpec((None, None, r, tn), lambda l, c, j: (l, c, 0, j)),
        out_shape=jax.ShapeDtypeStruct((depth, nmod, r, d), f32),
        compiler_params=_params(3), name="adaln")(c_all, w_ada, b_ada.reshape(depth, 1, m6))


class _Mod:
    def __init__(self, mod, layer, n_sample, rows_per_seq):
        self.mod, self.layer, self.n_sample, self.rows_per_seq = mod, layer, n_sample, rows_per_seq

    def operand(self, chunk, tm, tn, row_of, col_of):
        depth, nmod, r, d = self.mod.shape
        l, bd, rps = self.layer, self.n_sample, self.rows_per_seq
        if rps is None:
            assert tm == bd
            return self.mod, pl.BlockSpec((None, None, tm, tn), lambda *g: (l, chunk, 0, col_of(*g)))
        return (self.mod.reshape(depth, nmod, r, 1, d),
                pl.BlockSpec((None, None, None, 1, tn), lambda *g: (l, chunk, bd + (row_of(*g) * tm) // rps, 0, col_of(*g))))


def _modulated(x_ref, g_ref, sc_ref, sh_ref):
    x = x_ref[...]
    y = x * lax.rsqrt(jnp.mean(x * x, axis=-1, keepdims=True) + EPS) * g_ref[...]
    return y * (1.0 + sc_ref[...]) + sh_ref[...]


def _modulate_body(x_ref, g_ref, sc_ref, sh_ref, o_ref):
    o_ref[...] = _modulated(x_ref, g_ref, sc_ref, sh_ref).astype(o_ref.dtype)


def _route(logits, b_router):
    scores = jax.nn.sigmoid(logits)
    biased = scores + b_router
    lane_i = lax.broadcasted_iota(jnp.int32, logits.shape, 1)
    lane = lane_i.astype(f32)
    grp = lane_i >> 2
    neg = -jnp.inf

    def top2(v):
        t1 = jnp.max(v, axis=-1, keepdims=True)
        i1 = jnp.min(jnp.where(v == t1, lane, float(LANES)), axis=-1, keepdims=True)
        v2 = jnp.where(lane == i1, neg, v)
        t2 = jnp.max(v2, axis=-1, keepdims=True)
        i2 = jnp.min(jnp.where(v2 == t2, lane, float(LANES)), axis=-1, keepdims=True)
        return t1, i1, t2, i2

    best, best_g = None, None
    for g in range(N_EXPERTS // EXPERTS_PER_GROUP):
        t1, _, t2, _ = top2(jnp.where(grp == g, biased, neg))
        gs = t1 + t2
        if best is None:
            best, best_g = gs, jnp.zeros(gs.shape, jnp.int32)
        else:
            take = gs > best
            best_g = jnp.where(take, g, best_g)
            best = jnp.where(take, gs, best)
    _, i1, _, i2 = top2(jnp.where(grp == best_g, biased, neg))
    s1 = jnp.sum(jnp.where(lane == i1, scores, 0.0), axis=-1, keepdims=True)
    s2 = jnp.sum(jnp.where(lane == i2, scores, 0.0), axis=-1, keepdims=True)
    den = s1 + s2
    return jnp.where(lane == i1, s1 / den, 0.0) + jnp.where(lane == i2, s2 / den, 0.0), best_g


def _modulate_route_body(x_ref, g_ref, sc_ref, sh_ref, wr_ref, br_ref, o_ref, comb_ref):
    h = _modulated(x_ref, g_ref, sc_ref, sh_ref)
    d = h.shape[1]
    o_ref[:, :d] = h.astype(o_ref.dtype)
    logits = jnp.dot(h, wr_ref[...], precision=lax.Precision.HIGHEST, preferred_element_type=f32)
    comb, group = _route(logits, br_ref[...])
    lane = lax.broadcasted_iota(jnp.int32, comb.shape, 1)
    comb = jnp.where(lane == N_EXPERTS, group.astype(f32), comb)
    comb_ref[...] = comb
    if o_ref.shape[1] > d:
        o_ref[:, d:] = comb


def _modulate(x, norm_g, mod, sc_chunk, sh_chunk, tm, router=None, pack_weights=False):
    n, d = x.shape
    row, col = (lambda i: i), (lambda i: 0)
    sc, sc_spec = mod.operand(sc_chunk, tm, d, row, col)
    sh, sh_spec = mod.operand(sh_chunk, tm, d, row, col)
    in_specs = [pl.BlockSpec((tm, d), lambda i: (i, 0)), pl.BlockSpec((1, d), lambda i: (0, 0)), sc_spec, sh_spec]
    args = [x, norm_g.reshape(1, d), sc, sh]
    out_specs = pl.BlockSpec((tm, d), lambda i: (i, 0))
    out_shape = jax.ShapeDtypeStruct((n, d), bf16)
    body = _modulate_body
    if router is not None:
        w_router_pad, b_router_pad = router
        in_specs += [pl.BlockSpec((d, LANES), lambda i: (0, 0)), pl.BlockSpec((1, LANES), lambda i: (0, 0))]
        args += [w_router_pad, b_router_pad]
        width = d + LANES if pack_weights else d
        out_specs = [pl.BlockSpec((tm, width), lambda i: (i, 0)), pl.BlockSpec((tm, LANES), lambda i: (i, 0))]
        out_shape = [jax.ShapeDtypeStruct((n, width), f32 if pack_weights else bf16),
                     jax.ShapeDtypeStruct((n, LANES), f32)]
        body = _modulate_route_body
    return pl.pallas_call(body, grid=(n // tm,), in_specs=in_specs, out_specs=out_specs, out_shape=out_shape,
                          compiler_params=_params(1), name="modulate")(*args)


def _proj_body(epilogue, n_extra, h_ref, w_ref, *rest):
    extra, outs, wb_ref = rest[:n_extra], rest[n_extra:-1], rest[-1]

    @pl.when(pl.program_id(1) == 0)
    def _():
        wb_ref[...] = w_ref[...].T.astype(bf16)

    acc = jnp.dot(h_ref[...], wb_ref[...], preferred_element_type=f32)
    epilogue(acc, extra, outs)


def _heads(width):
    return [slice(h * HEAD_DIM, (h + 1) * HEAD_DIM) for h in range(width // HEAD_DIM)]


def _epi_rope(acc, extra, outs):
    cos, sin = extra[0][...], extra[1][...]
    kscale = jnp.where(pl.program_id(0) == 1, HEAD_DIM ** -0.5, 1.0).astype(f32)
    for hs in _heads(acc.shape[1]):
        a = acc[:, hs]
        outs[0][:, hs] = ((a * cos + pltpu.roll(a, HEAD_DIM // 2, 1) * sin) * kscale).astype(outs[0].dtype)


def _epi_headnorm(acc, extra, outs):
    g, post = extra[0][...], extra[1][...]
    for hs in _heads(acc.shape[1]):
        a = acc[:, hs]
        y = a * lax.rsqrt(jnp.mean(a * a, axis=-1, keepdims=True) + EPS) * g
        for o in outs[:-1]:
            o[:, hs] = y.astype(o.dtype)
        outs[-1][:, hs] = (y * post).astype(outs[-1].dtype)


def _epi_plain(acc, extra, outs):
    for o in outs:
        o[...] = acc.astype(o.dtype)


def _epi_forget(acc, extra, outs):
    y = _log_sigmoid(acc + extra[0][...])
    outs[0][...] = y[:, :outs[0].shape[1]]


def _proj(h16, w_t, layer, col0, ncol, tn, tm, epilogue, extra, out_dtypes, out_width=None):
    n, k = h16.shape
    out_width = tn if out_width is None else out_width
    in_specs = [pl.BlockSpec((tm, k), lambda j, i: (i, 0)),
                pl.BlockSpec((None, tn, k), lambda j, i: (layer, col0 + j, 0))] + [s for _, s in extra]
    out_specs = [pl.BlockSpec((tm, out_width), lambda j, i: (i, j)) for _ in out_dtypes]
    out_shape = [jax.ShapeDtypeStruct((n, ncol * out_width), dt) for dt in out_dtypes]
    return pl.pallas_call(
        functools.partial(_proj_body, epilogue, len(extra)), grid=(ncol, n // tm),
        in_specs=in_specs, out_specs=out_specs, out_shape=out_shape,
        scratch_shapes=[pltpu.VMEM((k, tn), bf16)],
        compiler_params=_params(2), name="proj_" + epilogue.__name__[5:])(h16, w_t, *[a for a, _ in extra])


def _group_norm_gate(o, gain, gate):
    mu = jnp.mean(o, axis=-1, keepdims=True)
    var = jnp.mean(jnp.square(o - mu), axis=-1, keepdims=True)
    return (o - mu) * lax.rsqrt(var + EPS) * gain * _silu(gate)


def _ret_prompt_body(q_ref, k_ref, v_ref, g_ref, intra_ref, crossd_ref, kdec_ref, sdec_ref, gn_ref, o_ref, st_ref):
    @pl.when(pl.program_id(1) == 0)
    def _():
        st_ref[...] = jnp.zeros(st_ref.shape, f32)

    for h, hs in enumerate(_heads(q_ref.shape[1])):
        q, k, v = q_ref[:, hs], k_ref[:, hs], v_ref[:, hs]
        state = st_ref[h]
        scores = lax.dot_general(q, k, _NT, preferred_element_type=f32) * intra_ref[h]
        o = jnp.dot(scores.astype(bf16), v, preferred_element_type=f32)
        o = o + jnp.dot(q, state.astype(bf16), preferred_element_type=f32) * crossd_ref[h]
        k_dec = (k.astype(f32) * kdec_ref[h]).astype(bf16)
        st_ref[h] = state * sdec_ref[h] + lax.dot_general(k_dec, v, _TN, preferred_element_type=f32)
        o_ref[:, hs] = _group_norm_gate(o, gn_ref[:, hs], g_ref[:, hs].astype(f32)).astype(o_ref.dtype)


def _ret_tables():
    lg = jnp.log1p(-jnp.exp2(-5.0 - jnp.arange(N_HEADS, dtype=f32)))[:, None, None]
    t = jnp.arange(RET_CHUNK, dtype=f32)
    diff = t[None, :, None] - t[None, None, :]
    intra = jnp.where(diff >= 0, jnp.exp(lg * jnp.maximum(diff, 0.0)), 0.0)
    ones = jnp.ones((1, 1, HEAD_DIM), f32)
    crossd = jnp.exp(lg * (t[None, :, None] + 1.0)) * ones
    kdec = jnp.exp(lg * (RET_CHUNK - 1.0 - t[None, :, None])) * ones
    sdec = jnp.exp(lg * RET_CHUNK) * ones
    step_dec = jnp.exp(lg) * ones
    return intra, crossd, kdec, sdec, step_dec


def _ret_prompt(qk16, vg16, ret_norm_g, batch, tables):
    n = qk16.shape[0]
    nc = n // batch // RET_CHUNK
    intra, crossd, kdec, sdec, _ = tables
    row = lambda b, c: b * nc + c
    full3 = lambda a: pl.BlockSpec(a.shape, lambda b, c: (0, 0, 0))
    blk = lambda col: pl.BlockSpec((RET_CHUNK, GROUP_WIDTH), lambda b, c: (row(b, c), col))
    return pl.pallas_call(
        _ret_prompt_body, grid=(batch, nc),
        in_specs=[blk(0), blk(1), blk(0), blk(1), full3(intra), full3(crossd), full3(kdec), full3(sdec),
                  pl.BlockSpec((1, GROUP_WIDTH), lambda b, c: (0, 0))],
        out_specs=[blk(0), pl.BlockSpec((None, N_HEADS, HEAD_DIM, HEAD_DIM), lambda b, c: (b, 0, 0, 0))],
        out_shape=[jax.ShapeDtypeStruct((n, GROUP_WIDTH), bf16),
                   jax.ShapeDtypeStruct((batch, N_HEADS, HEAD_DIM, HEAD_DIM), f32)],
        compiler_params=_params(2), name="ret_prompt")(
            qk16, qk16, vg16, vg16, intra, crossd, kdec, sdec, ret_norm_g.reshape(1, GROUP_WIDTH))


def _block_diag(x):
    xt = jnp.tile(x, (1, N_HEADS))
    lane = lax.broadcasted_iota(jnp.int32, xt.shape, 1)
    row = lax.broadcasted_iota(jnp.int32, xt.shape, 0)
    return jnp.where((lane >> 7) == row, xt, 0.0)


RET_SAMPLE_SEQS = 4


def _ret_sample_body(q_ref, k_ref, v_ref, g_ref, st_ref, dec3_ref, dec2_ref, gn_ref, o_ref, sn_ref):
    for i in range(q_ref.shape[0]):
        q16, k16, v16 = q_ref[i].astype(bf16), k_ref[i].astype(bf16), v_ref[i].astype(bf16)
        q, k, v = q16.astype(f32), k16.astype(f32), v16.astype(f32)
        state = st_ref[i]
        cross = jnp.dot(_block_diag(q).astype(bf16), state.reshape(GROUP_WIDTH, HEAD_DIM).astype(bf16),
                        preferred_element_type=f32) * dec2_ref[...]
        qk = jnp.sum(q * k, axis=-1, keepdims=True).astype(bf16).astype(f32)
        o = qk * v + cross
        upd = lax.dot_general(_block_diag(k).astype(bf16), v16, _TN, preferred_element_type=f32)
        sn_ref[i] = state * dec3_ref[...] + upd.reshape(state.shape)
        o_ref[i] = _group_norm_gate(o, gn_ref[...], g_ref[i]).astype(o_ref.dtype)


def _ret_sample(q3, k3, v3, g3, state_ret, layer, ret_norm_g, step_dec):
    bd = q3.shape[0]
    nb = RET_SAMPLE_SEQS if bd % RET_SAMPLE_SEQS == 0 else 1
    tok = pl.BlockSpec((nb, N_HEADS, HEAD_DIM), lambda b: (b, 0, 0))
    return pl.pallas_call(
        _ret_sample_body, grid=(bd // nb,),
        in_specs=[tok, tok, tok, tok,
                  pl.BlockSpec((None, nb, N_HEADS, HEAD_DIM, HEAD_DIM), lambda b: (layer, b, 0, 0, 0)),
                  pl.BlockSpec(step_dec.shape, lambda b: (0, 0, 0)),
                  pl.BlockSpec((N_HEADS, HEAD_DIM), lambda b: (0, 0)),
                  pl.BlockSpec((N_HEADS, HEAD_DIM), lambda b: (0, 0))],
        out_specs=[tok, pl.BlockSpec((nb, N_HEADS, HEAD_DIM, HEAD_DIM), lambda b: (b, 0, 0, 0))],
        out_shape=[jax.ShapeDtypeStruct((bd, N_HEADS, HEAD_DIM), f32),
                   jax.ShapeDtypeStruct((bd, N_HEADS, HEAD_DIM, HEAD_DIM), f32)],
        compiler_params=_params(1), name="ret_sample")(
            q3, k3, v3, g3, state_ret, step_dec, step_dec.reshape(N_HEADS, HEAD_DIM),
            ret_norm_g.reshape(N_HEADS, HEAD_DIM))


def _cumsum_body(x_ref, o_ref, carry_ref):
    @pl.when(pl.program_id(1) == 0)
    def _():
        carry_ref[...] = jnp.zeros(carry_ref.shape, f32)

    tb = x_ref.shape[1]
    upper = (lax.broadcasted_iota(jnp.int32, (tb, tb), 0) <= lax.broadcasted_iota(jnp.int32, (tb, tb), 1)).astype(f32)
    cum = jnp.dot(x_ref[...], upper, precision=lax.Precision.HIGHEST, preferred_element_type=f32) + carry_ref[:, :1]
    o_ref[...] = cum
    carry_ref[...] = jnp.broadcast_to(cum[:, tb - 1:tb], carry_ref.shape)


def _cumsum_lanes(x_t, tb):
    b, h, s = x_t.shape
    spec = pl.BlockSpec((None, h, tb), lambda i, j: (i, 0, j))
    return pl.pallas_call(_cumsum_body, grid=(b, s // tb), in_specs=[spec], out_specs=spec,
                          out_shape=jax.ShapeDtypeStruct(x_t.shape, f32),
                          scratch_shapes=[pltpu.VMEM((h, LANES), f32)],
                          compiler_params=_params(2), name="forget_cumsum")(x_t)


LOG2E = 1.4426950408889634
HEADS_PER_STEP = 2


def _forget_tail(f_sel, first_lane, sign, ones_lane):
    x = f_sel * (sign * LOG2E)
    hi = x.astype(bf16).astype(f32)
    rest = x - hi
    mid = rest.astype(bf16).astype(f32)
    lo = (rest - mid).astype(bf16).astype(f32)
    lane = lax.broadcasted_iota(jnp.int32, (f_sel.shape[0], HEAD_DIM), 1)
    ones = jnp.where((lane >= ones_lane) & (lane < ones_lane + 3), 1.0, 0.0)
    tail = jnp.where(lane == first_lane, hi, jnp.where(lane == first_lane + 1, mid,
                                                      jnp.where(lane == first_lane + 2, lo, ones)))
    return tail.astype(bf16)


def _select_head(f_block, head):
    lane = lax.broadcasted_iota(jnp.int32, f_block.shape, 1)
    return jnp.sum(jnp.where(lane == head, f_block, 0.0), axis=-1, keepdims=True)


def _fox_prompt_body(q_ref, k_ref, v_ref, fq_ref, fk_ref, o_ref, qaug_ref, kaug_ref, m_ref, l_ref, acc_ref):
    pair, qi = pl.program_id(1), pl.program_id(2)
    tq = q_ref.shape[0]
    n_sub = q_ref.shape[1] // HEAD_DIM
    seq = k_ref.shape[0]

    @pl.when(qi == 0)
    def _():
        def fill(c, carry):
            rows = pl.ds(pl.multiple_of(c * tq, tq), tq)
            for hh, hs in enumerate(_heads(q_ref.shape[1])):
                kaug_ref[hh, rows, :HEAD_DIM] = k_ref[rows, hs]
                kaug_ref[hh, rows, HEAD_DIM:] = _forget_tail(_select_head(fk_ref[rows, :], pair * n_sub + hh), 3, -1.0, 0)
            return carry
        lax.fori_loop(0, seq // tq, fill, 0)

    for hh, hs in enumerate(_heads(q_ref.shape[1])):
        qaug_ref[hh, :, :HEAD_DIM] = q_ref[:, hs]
        qaug_ref[hh, :, HEAD_DIM:] = _forget_tail(_select_head(fq_ref[...], pair * n_sub + hh), 0, 1.0, 3)
    m_ref[...] = jnp.full(m_ref.shape, NEG_BIG, f32)
    l_ref[...] = jnp.zeros(l_ref.shape, f32)
    acc_ref[...] = jnp.zeros(acc_ref.shape, f32)

    def step(j, diagonal):
        rows = pl.ds(pl.multiple_of(j * tq, tq), tq)
        for hh, hs in enumerate(_heads(q_ref.shape[1])):
            s = lax.dot_general(qaug_ref[hh], kaug_ref[hh, rows, :], _NT, preferred_element_type=f32)
            if diagonal:
                r = lax.broadcasted_iota(jnp.int32, s.shape, 0)
                c = lax.broadcasted_iota(jnp.int32, s.shape, 1)
                s = jnp.where(c <= r, s, -jnp.inf)
            m_old = m_ref[hh]
            m_new = jnp.maximum(m_old, jnp.max(s, axis=-1, keepdims=True))
            alpha = jnp.exp2(m_old - m_new)
            p = jnp.exp2(s - jnp.tile(m_new, (1, tq // LANES)))
            l_ref[hh] = alpha * l_ref[hh] + jnp.sum(p, axis=-1, keepdims=True)
            acc_ref[hh] = alpha * acc_ref[hh] + jnp.dot(p.astype(bf16), v_ref[rows, hs], preferred_element_type=f32)
            m_ref[hh] = m_new

    def off_diagonal(j, carry):
        step(j, False)
        return carry

    lax.fori_loop(0, qi, off_diagonal, 0)
    step(qi, True)
    for hh, hs in enumerate(_heads(q_ref.shape[1])):
        o_ref[:, hs] = (acc_ref[hh] / l_ref[hh]).astype(o_ref.dtype)


def _fox_prompt(q16, k16, v16, f_cols, batch, tq):
    n = q16.shape[0]
    s = n // batch
    nq = s // tq
    width = HEADS_PER_STEP * HEAD_DIM
    kv = lambda a: a.reshape(batch, s, GROUP_WIDTH)
    kv_spec = pl.BlockSpec((None, s, width), lambda b, h, i: (b, 0, h))
    q_spec = pl.BlockSpec((tq, width), lambda b, h, i: (b * nq + i, h))
    sub = (HEADS_PER_STEP, tq, HEAD_DIM)
    return pl.pallas_call(
        _fox_prompt_body, grid=(batch, N_HEADS // HEADS_PER_STEP, nq),
        in_specs=[q_spec, kv_spec, kv_spec,
                  pl.BlockSpec((None, tq, N_HEADS), lambda b, h, i: (b, i, 0)),
                  pl.BlockSpec((None, s, N_HEADS), lambda b, h, i: (b, 0, 0))],
        out_specs=q_spec,
        out_shape=jax.ShapeDtypeStruct((n, GROUP_WIDTH), bf16),
        scratch_shapes=[pltpu.VMEM((HEADS_PER_STEP, tq, 2 * HEAD_DIM), bf16),
                        pltpu.VMEM((HEADS_PER_STEP, s, 2 * HEAD_DIM), bf16),
                        pltpu.VMEM(sub, f32), pltpu.VMEM(sub, f32), pltpu.VMEM(sub, f32)],
        compiler_params=_params(3), name="fox_prompt")(q16, kv(k16), kv(v16), f_cols, f_cols)


FLAT = PAGE_SIZE * N_HEADS
PAGES_PER_STEP = 8


def _dot_select(y, m01):
    hi = y.astype(bf16)
    rest = y - hi.astype(f32)
    mid = rest.astype(bf16)
    lo = (rest - mid.astype(f32)).astype(bf16)
    return sum(jnp.dot(piece, m01, preferred_element_type=f32) for piece in (hi, mid, lo))


def _forget_pages_body(x_ref, excl_ref, tot_ref):
    x = x_ref[...]
    t = x.shape[1]
    later = (lax.broadcasted_iota(jnp.int32, (t, t), 0) > lax.broadcasted_iota(jnp.int32, (t, t), 1)).astype(bf16)
    excl_ref[...] = _dot_select(x, later)
    tot_ref[...] = jnp.broadcast_to(jnp.sum(x, axis=-1, keepdims=True), x.shape)


def _forget_pages(cache_logf):
    depth, n_phys, page, heads = cache_logf.shape
    assert page == PAGE_SIZE and heads == N_HEADS
    n = depth * n_phys
    assert n % SUBLANES == 0
    nr = n * heads
    tr = next((c for c in (2048, 1024, 512, 256, 128, 64) if nr % c == 0), nr)
    rows = cache_logf.transpose(0, 1, 3, 2).reshape(nr, page)
    spec = pl.BlockSpec((tr, page), lambda i: (i, 0))
    out = jax.ShapeDtypeStruct((nr, page), f32)
    excl, tot = pl.pallas_call(_forget_pages_body, grid=(nr // tr,), in_specs=[spec], out_specs=[spec, spec],
                               out_shape=[out, out], compiler_params=_params(1), name="forget_pages")(rows)
    excl_flat = excl.reshape(n, heads, page).transpose(0, 2, 1).reshape(n, FLAT)
    tot_flat = jnp.tile(tot[:, 0].reshape(n, heads), (1, page))
    return excl_flat, tot_flat


def _fox_sample_body(n_group, page_row0, pt_ref, q_ref, kn_ref, vn_ref, lnew_ref, *rest):
    k_refs, v_refs = rest[:n_group], rest[n_group:2 * n_group]
    excl_refs, tot_refs = rest[2 * n_group:3 * n_group], rest[3 * n_group:4 * n_group]
    o_ref, carry_ref, m_ref, l_ref, acc_ref = rest[4 * n_group:]
    step = pl.program_id(1)
    n_pages = pl.num_programs(1) * n_group
    scale = HEAD_DIM ** -0.5

    @pl.when(step == 0)
    def _():
        carry_ref[...] = lnew_ref[...]
        m_ref[...] = jnp.full(m_ref.shape, NEG_BIG, f32)
        l_ref[...] = jnp.zeros(l_ref.shape, f32)
        acc_ref[...] = jnp.zeros(acc_ref.shape, f32)

    q16 = q_ref[...].astype(bf16)
    own_head = ((lax.broadcasted_iota(jnp.int32, (N_HEADS, FLAT), 1) & (N_HEADS - 1))
                == lax.broadcasted_iota(jnp.int32, (N_HEADS, FLAT), 0))
    carry = carry_ref[...]
    scores = []
    for j in range(n_group):
        kf = k_refs[j][...].reshape(FLAT, HEAD_DIM).astype(bf16)
        row = pl.ds((page_row0 + pt_ref[pl.program_id(0), n_pages - 1 - (step * n_group + j)]) % SUBLANES, 1)
        s = lax.dot_general(q16, kf, _NT, preferred_element_type=f32) * scale + (carry + excl_refs[j][row, :])
        scores.append(jnp.where(own_head, s, -jnp.inf))
        carry = carry + tot_refs[j][row, :]
    carry_ref[...] = carry

    top = scores[0]
    for s in scores[1:]:
        top = jnp.maximum(top, s)
    m_old = m_ref[...]
    m_new = jnp.maximum(m_old, jnp.max(top, axis=-1, keepdims=True))
    alpha = jnp.exp(m_old - m_new)
    l_new = alpha * l_ref[...]
    acc = alpha * acc_ref[...]
    for j in range(n_group):
        pe = jnp.exp(scores[j] - m_new)
        l_new = l_new + jnp.sum(pe, axis=-1, keepdims=True)
        vf = v_refs[j][...].reshape(FLAT, HEAD_DIM).astype(bf16)
        acc = acc + jnp.dot(pe.astype(bf16), vf, preferred_element_type=f32)
    m_ref[...] = m_new
    l_ref[...] = l_new
    acc_ref[...] = acc

    @pl.when(step == pl.num_programs(1) - 1)
    def _():
        rounded = lambda ref: ref[...].astype(bf16).astype(f32)
        s_new = jnp.sum(rounded(q_ref) * rounded(kn_ref), axis=-1, keepdims=True) * scale
        m_fin = jnp.maximum(m_new, s_new)
        beta = jnp.exp(m_new - m_fin)
        p_new = jnp.exp(s_new - m_fin)
        num = beta * acc + p_new.astype(bf16).astype(f32) * rounded(vn_ref)
        o_ref[...] = (num / (beta * l_new + p_new)).astype(o_ref.dtype)


def _fox_sample(q3, kn3, vn3, logf_new, cache_k, cache_v, excl_flat, tot_flat, layer, page_table):
    bd, n_pages = page_table.shape
    n_phys = cache_k.shape[1]
    g = PAGES_PER_STEP if n_pages % PAGES_PER_STEP == 0 else 1
    lnew_flat = jnp.tile(logf_new, (1, PAGE_SIZE)).reshape(bd, 1, FLAT)
    tok = pl.BlockSpec((None, N_HEADS, HEAD_DIM), lambda b, s, pt: (b, 0, 0))

    def page_of(j):
        return lambda b, s, pt: pt[b, n_pages - 1 - (s * g + j)]

    def kv_spec(j):
        pg = page_of(j)
        return pl.BlockSpec((None, None, PAGE_SIZE, N_HEADS, HEAD_DIM), lambda b, s, pt: (layer, pg(b, s, pt), 0, 0, 0))

    def flat_spec(j):
        pg = page_of(j)
        return pl.BlockSpec((SUBLANES, FLAT), lambda b, s, pt: ((layer * n_phys + pg(b, s, pt)) // SUBLANES, 0))

    groups = [[kv_spec(j) for j in range(g)], [kv_spec(j) for j in range(g)],
              [flat_spec(j) for j in range(g)], [flat_spec(j) for j in range(g)]]
    grid_spec = pltpu.PrefetchScalarGridSpec(
        num_scalar_prefetch=1, grid=(bd, n_pages // g),
        in_specs=[tok, tok, tok, pl.BlockSpec((None, 1, FLAT), lambda b, s, pt: (b, 0, 0))] + sum(groups, []),
        out_specs=tok,
        scratch_shapes=[pltpu.VMEM((1, FLAT), f32), pltpu.VMEM((N_HEADS, 1), f32), pltpu.VMEM((N_HEADS, 1), f32),
                        pltpu.VMEM((N_HEADS, HEAD_DIM), f32)])
    return pl.pallas_call(
        functools.partial(_fox_sample_body, g, layer * n_phys), grid_spec=grid_spec,
        out_shape=jax.ShapeDtypeStruct((bd, N_HEADS, HEAD_DIM), f32),
        compiler_params=_params(2), name="fox_sample")(
            page_table, q3, kn3, vn3, lnew_flat, *([cache_k] * g + [cache_v] * g + [excl_flat] * g + [tot_flat] * g))


def _out_proj_body(ret_ref, fox_ref, w_ref, x_ref, ga_ref, o_ref, wb_ref):
    @pl.when(pl.program_id(1) == 0)
    def _():
        wb_ref[...] = w_ref[...].astype(bf16)

    half = ret_ref.shape[1]
    y = jnp.dot(ret_ref[...].astype(bf16), wb_ref[:half, :], preferred_element_type=f32)
    y = y + jnp.dot(fox_ref[...].astype(bf16), wb_ref[half:, :], preferred_element_type=f32)
    o_ref[...] = x_ref[...] + ga_ref[...] * y


def _out_proj(ret16, fox16, w_out, layer, x, mod, tm, tn):
    n, d = x.shape
    k = w_out.shape[1]
    ga, ga_spec = mod.operand(2, tm, tn, lambda j, i: i, lambda j, i: j)
    act = pl.BlockSpec((tm, ret16.shape[1]), lambda j, i: (i, 0))
    xs = pl.BlockSpec((tm, tn), lambda j, i: (i, j))
    return pl.pallas_call(
        _out_proj_body, grid=(d // tn, n // tm),
        in_specs=[act, act, pl.BlockSpec((None, k, tn), lambda j, i: (layer, 0, j)), xs, ga_spec],
        out_specs=xs, out_shape=jax.ShapeDtypeStruct((n, d), f32),
        scratch_shapes=[pltpu.VMEM((k, tn), bf16)],
        compiler_params=_params(2), name="out_proj")(ret16, fox16, w_out, x, ga)


def _expert(h, wg_ref, wu_ref, wd_ref):
    g = jnp.dot(h, wg_ref[...].astype(bf16), preferred_element_type=f32)
    u = jnp.dot(h, wu_ref[...].astype(bf16), preferred_element_type=f32)
    return jnp.dot((_silu(g) * u).astype(bf16), wd_ref[...].astype(bf16), preferred_element_type=f32)


def _moe_body(h_ref, comb_ref, wg_ref, wu_ref, wd_ref, x_ref, gm_ref, o_ref, acc_ref):
    e = pl.program_id(1)

    @pl.when(e == 0)
    def _():
        acc_ref[...] = jnp.zeros(acc_ref.shape, f32)

    y = _expert(h_ref[...], wg_ref, wu_ref, wd_ref)
    lane = lax.broadcasted_iota(jnp.int32, comb_ref.shape, 1)
    w = jnp.sum(jnp.where(lane == e, comb_ref[...], 0.0), axis=-1, keepdims=True)
    acc_ref[...] += w * y

    @pl.when(e == pl.num_programs(1) - 1)
    def _():
        o_ref[...] = x_ref[...] + gm_ref[...] * acc_ref[...]


def _moe(h16, comb, wg, wu, wd, layer, x, mod, tm):
    n, d = x.shape
    n_exp, _, de = wg.shape[1:]
    gm, gm_spec = mod.operand(5, tm, d, lambda i, e: i, lambda i, e: 0)
    rows = lambda w: pl.BlockSpec((tm, w), lambda i, e: (i, 0))
    return pl.pallas_call(
        _moe_body, grid=(n // tm, n_exp),
        in_specs=[rows(d), rows(LANES),
                  pl.BlockSpec((None, None, d, de), lambda i, e: (layer, e, 0, 0)),
                  pl.BlockSpec((None, None, d, de), lambda i, e: (layer, e, 0, 0)),
                  pl.BlockSpec((None, None, de, d), lambda i, e: (layer, e, 0, 0)),
                  rows(d), gm_spec],
        out_specs=rows(d), out_shape=jax.ShapeDtypeStruct((n, d), f32),
        scratch_shapes=[pltpu.VMEM((tm, d), f32)],
        compiler_params=_params(2), name="moe")(h16, comb, wg, wu, wd, x, gm)


N_GROUPS = N_EXPERTS // EXPERTS_PER_GROUP
ROW_DMA_UNROLL = 8


def _group_layout(group, tile):
    n = group.shape[0]
    onehot = (group[:, None] == jnp.arange(N_GROUPS, dtype=jnp.int32)[None, :]).astype(jnp.int32)
    running = jnp.cumsum(onehot, axis=0)
    rank = jnp.sum(onehot * running, axis=1) - 1
    padded = (running[-1] + tile - 1) // tile * tile
    ends = jnp.cumsum(padded)
    slot = jnp.sum(onehot * (ends - padded)[None, :], axis=1) + rank
    starts = jnp.arange(n // tile + N_GROUPS, dtype=jnp.int32) * tile
    tile_group = jnp.minimum(jnp.sum((starts[:, None] >= ends[None, :]).astype(jnp.int32), axis=1), N_GROUPS - 1)
    return slot.astype(jnp.int32), tile_group.astype(jnp.int32), (ends[-1] // tile).astype(jnp.int32).reshape(1)


def _start_rows(n_rows, copy_of):
    def start(c, carry):
        for u in range(ROW_DMA_UNROLL):
            copy_of(c * ROW_DMA_UNROLL + u).start(priority=u % 2)
        return carry

    lax.fori_loop(0, n_rows // ROW_DMA_UNROLL, start, 0)


def _wait_rows(n_rows, copy_of):
    def wait(c, carry):
        for u in range(ROW_DMA_UNROLL):
            copy_of(c * ROW_DMA_UNROLL + u).wait()
        return carry

    lax.fori_loop(0, n_rows // ROW_DMA_UNROLL, wait, 0)


def _moe_group_body(tile_rows, slot_ref, tg_ref, used_ref, hx_hbm, wg_ref, wu_ref, wd_ref, o_ref,
                    rows_ref, source_ref, sems):
    tile, e4 = pl.program_id(0), pl.program_id(1)
    d = wg_ref.shape[0]
    used = used_ref[0]

    def gather(t):
        return lambda k: pltpu.make_async_copy(hx_hbm.at[pl.ds(source_ref[t * tile_rows + k], 1)],
                                               rows_ref.at[t % 2, pl.ds(k, 1)], sems.at[t % 2])

    @pl.when((tile == 0) & (e4 == 0))
    def _():
        def clear(c, carry):
            for u in range(ROW_DMA_UNROLL):
                source_ref[c * ROW_DMA_UNROLL + u] = 0
            return carry

        def place(c, carry):
            for u in range(ROW_DMA_UNROLL):
                source_ref[slot_ref[c * ROW_DMA_UNROLL + u]] = c * ROW_DMA_UNROLL + u
            return carry

        lax.fori_loop(0, source_ref.shape[0] // ROW_DMA_UNROLL, clear, 0)
        lax.fori_loop(0, slot_ref.shape[0] // ROW_DMA_UNROLL, place, 0)
        _start_rows(tile_rows, gather(0))

    @pl.when((e4 == 1) & (tile + 1 < used))
    def _():
        _start_rows(tile_rows, gather(tile + 1))

    @pl.when(e4 == 0)
    def _():
        o_ref[...] = jnp.zeros(o_ref.shape, f32)

    @pl.when((e4 == 0) & (tile < used))
    def _():
        _wait_rows(tile_rows, gather(tile))

    @pl.when(tile < used)
    def _():
        rows = rows_ref.at[tile % 2]
        y = _expert(rows[:, :d].astype(bf16), wg_ref, wu_ref, wd_ref)
        e = tg_ref[tile] * EXPERTS_PER_GROUP + e4
        comb = rows[:, d:]
        lane = lax.broadcasted_iota(jnp.int32, comb.shape, 1)
        o_ref[...] += jnp.sum(jnp.where(lane == e, comb, 0.0), axis=-1, keepdims=True) * y


def _moe_group(hx, slot, tile_group, n_used, wg, wu, wd, layer, tile):
    n_slots = hx.shape[0] + N_GROUPS * tile
    d, de = wg.shape[2:]
    assert hx.shape[0] % ROW_DMA_UNROLL == 0 and tile % ROW_DMA_UNROLL == 0

    def expert(i, e4, sl, tg, used):
        last = used[0] - 1
        return jnp.where(i <= last, tg[i] * EXPERTS_PER_GROUP + e4, tg[last] * EXPERTS_PER_GROUP + EXPERTS_PER_GROUP - 1)

    weights = lambda rows, cols: pl.BlockSpec((None, None, rows, cols),
                                              lambda i, e4, sl, tg, used: (layer, expert(i, e4, sl, tg, used), 0, 0))
    grid_spec = pltpu.PrefetchScalarGridSpec(
        num_scalar_prefetch=3, grid=(n_slots // tile, EXPERTS_PER_GROUP),
        in_specs=[pl.BlockSpec(memory_space=pl.ANY), weights(d, de), weights(d, de), weights(de, d)],
        out_specs=pl.BlockSpec((tile, d), lambda i, e4, sl, tg, used: (i, 0)),
        scratch_shapes=[pltpu.VMEM((2, tile, hx.shape[1]), f32), pltpu.SMEM((n_slots,), jnp.int32),
                        pltpu.SemaphoreType.DMA((2,))])
    return pl.pallas_call(functools.partial(_moe_group_body, tile), grid_spec=grid_spec,
                          out_shape=jax.ShapeDtypeStruct((n_slots, d), f32),
                          compiler_params=_params(2), name="moe_group")(slot, tile_group, n_used, hx, wg, wu, wd)


def _moe_combine_body(slot_ref, x_ref, gm_ref, ys_hbm, o_ref, rows_ref, sem):
    tm = x_ref.shape[0]
    base = pl.program_id(0) * tm
    fetch = lambda k: pltpu.make_async_copy(ys_hbm.at[pl.ds(slot_ref[base + k], 1)], rows_ref.at[pl.ds(k, 1)], sem)
    _start_rows(tm, fetch)
    _wait_rows(tm, fetch)
    o_ref[...] = x_ref[...] + gm_ref[...] * rows_ref[...]


def _moe_combine(x, mod, ys, slot, tm):
    n, d = x.shape
    gm, gm_spec = mod.operand(5, tm, d, lambda i, s: i, lambda i, s: 0)
    rows = pl.BlockSpec((tm, d), lambda i, s: (i, 0))
    grid_spec = pltpu.PrefetchScalarGridSpec(
        num_scalar_prefetch=1, grid=(n // tm,),
        in_specs=[rows, gm_spec, pl.BlockSpec(memory_space=pl.ANY)], out_specs=rows,
        scratch_shapes=[pltpu.VMEM((tm, d), f32), pltpu.SemaphoreType.DMA(())])
    return pl.pallas_call(_moe_combine_body, grid_spec=grid_spec, out_shape=jax.ShapeDtypeStruct((n, d), f32),
                          compiler_params=_params(1), name="moe_combine")(slot, x, gm, ys)


def _rope_tables(positions):
    half = HEAD_DIM // 2
    inv_freq = ROPE_BASE ** (-jnp.arange(half, dtype=f32) / half)
    ang = positions.astype(f32)[:, None] * inv_freq[None, :]
    cos, sin = jnp.cos(ang), jnp.sin(ang)
    return jnp.concatenate([cos, cos], axis=-1), jnp.concatenate([-sin, sin], axis=-1)


def _mixer_in(x, mod, layer, tm, pos_tables, pos_tiles, p, lo, q_scale):
    h16 = _modulate(x, p["norm_mix_g"][layer], mod, 1, 0, tm)
    cos, sin = pos_tables
    pos_spec = pl.BlockSpec((tm, HEAD_DIM), lambda j, i: (i % pos_tiles, 0))
    w_in = p["w_in_t"]
    gw = GROUP_WIDTH
    (qk16,) = _proj(h16, w_in, layer, 0, 2, gw, tm, _epi_rope, [(cos, pos_spec), (sin, pos_spec)], [lo])
    (vg16,) = _proj(h16, w_in, layer, 2, 2, gw, tm, _epi_plain, [], [lo])
    row = lambda a: (a.reshape(1, HEAD_DIM), pl.BlockSpec((1, HEAD_DIM), lambda j, i: (0, 0)))
    const = lambda c: row(jnp.full((HEAD_DIM,), c, f32))
    (fq16,) = _proj(h16, w_in, layer, 4, 1, gw, tm, _epi_headnorm, [row(p["q_norm_g"][layer]), const(q_scale)], [lo])
    fk32, fk16 = _proj(h16, w_in, layer, 5, 1, gw, tm, _epi_headnorm, [row(p["k_norm_g"][layer]), const(1.0)], [f32, lo])
    fv32, fv16 = _proj(h16, w_in, layer, 6, 1, gw, tm, _epi_plain, [], [f32, lo])
    n_forget = p["b_forget"].shape[1]
    w_f = jnp.pad(w_in[layer, 7 * gw:, :], ((0, LANES - n_forget), (0, 0)))[None]
    b_f = jnp.pad(p["b_forget"][layer], (0, LANES - n_forget)).reshape(1, LANES)
    (logf,) = _proj(h16, w_f, 0, 0, 1, LANES, tm, _epi_forget,
                    [(b_f, pl.BlockSpec((1, LANES), lambda j, i: (0, 0)))], [f32], out_width=n_forget)
    return qk16, vg16, fq16, fk32, fk16, fv32, fv16, logf


def _channel(x, mod, layer, tm, p, grouped):
    router = (p["w_router_pad"], p["b_router_pad"])
    experts = (p["w_gate"], p["w_up"], p["w_down"])
    if not grouped:
        h16, comb = _modulate(x, p["norm_ffn_g"][layer], mod, 4, 3, tm, router=router)
        return _moe(h16, comb, *experts, layer, x, mod, tm)
    hx, comb = _modulate(x, p["norm_ffn_g"][layer], mod, 4, 3, tm, router=router, pack_weights=True)
    slot, tile_group, n_used = _group_layout(comb[:, N_EXPERTS].astype(jnp.int32), tm)
    ys = _moe_group(hx, slot, tile_group, n_used, *experts, layer, tm)
    return _moe_combine(x, mod, ys, slot, tm)


def kernel(x_prompt, x_sample, cache_k, cache_v, cache_logf, state_ret, page_table, c_prompt, c_sample, w_ada, b_ada,
           norm_mix_g, norm_ffn_g, w_in, b_forget, q_norm_g, k_norm_g, ret_norm_g, w_out, w_router, b_router,
           w_gate, w_up, w_down):
    batch, seq, d = x_prompt.shape
    bd, t_new, _ = x_sample.shape
    assert t_new == 1, "one new token per sampled sequence"
    assert bd % 8 == 0 and d == 2 * GROUP_WIDTH
    depth = w_in.shape[0]
    n_pages = page_table.shape[1]
    past_len = n_pages * PAGE_SIZE
    n_p = batch * seq
    tm_p = min(512, seq)
    tq = min(512, seq)

    p = dict(norm_mix_g=norm_mix_g, norm_ffn_g=norm_ffn_g, w_in_t=w_in.transpose(0, 2, 1), b_forget=b_forget,
             q_norm_g=q_norm_g, k_norm_g=k_norm_g,
             w_router_pad=jnp.pad(w_router, ((0, 0), (0, LANES - w_router.shape[1]))),
             b_router_pad=jnp.pad(b_router, (0, LANES - b_router.shape[0])).reshape(1, LANES),
             w_gate=w_gate, w_up=w_up, w_down=w_down)

    r_rows = -(-(bd + batch) // 8) * 8
    c_all = jnp.concatenate([c_sample, c_prompt, jnp.zeros((r_rows - bd - batch, d), f32)], axis=0)
    mod_all = _adaln(c_all, w_ada, b_ada)

    excl_flat, tot_flat = _forget_pages(cache_logf)
    tables = _ret_tables()
    rope_p = _rope_tables(jnp.arange(seq))
    rope_s = _rope_tables(jnp.full((bd,), past_len))

    xp = x_prompt.reshape(n_p, d)
    xs = x_sample.reshape(bd, d)
    outs = [[] for _ in range(8)]
    tok3 = lambda a: a.reshape(bd, N_HEADS, HEAD_DIM)
    for l in range(depth):
        mod = _Mod(mod_all, l, bd, seq)
        qk16, vg16, fq16, fk32, fk16, fv32, fv16, logf = _mixer_in(xp, mod, l, tm_p, rope_p, seq // tm_p, p, bf16,
                                                                       LOG2E * HEAD_DIM ** -0.5)
        ret16, s_fin = _ret_prompt(qk16, vg16, ret_norm_g[l], batch, tables)
        logf_b = logf.reshape(batch, seq, -1)
        f_rows = _cumsum_lanes(logf_b.transpose(0, 2, 1), tq)
        fox16 = _fox_prompt(fq16, fk16, fv16, f_rows.transpose(0, 2, 1), batch, tq)
        xp = _out_proj(ret16, fox16, w_out, l, xp, mod, tm_p, GROUP_WIDTH)
        xp = _channel(xp, mod, l, tm_p, p, grouped=True)
        outs[0].append(fk32.reshape(batch, seq, N_HEADS, HEAD_DIM))
        outs[1].append(fv32.reshape(batch, seq, N_HEADS, HEAD_DIM))
        outs[2].append(logf_b)
        outs[3].append(s_fin)
        mod = _Mod(mod_all, l, bd, None)
        qk16, vg16, fq16, fk32, fk16, fv32, fv16, logf = _mixer_in(xs, mod, l, bd, rope_s, 1, p, f32, 1.0)
        ret3, s_new = _ret_sample(tok3(qk16[:, :GROUP_WIDTH]), tok3(qk16[:, GROUP_WIDTH:]), tok3(vg16[:, :GROUP_WIDTH]),
                                  tok3(vg16[:, GROUP_WIDTH:]), state_ret, l, ret_norm_g[l], tables[4])
        fox3 = _fox_sample(tok3(fq16), tok3(fk16), tok3(fv16), logf, cache_k, cache_v, excl_flat, tot_flat, l, page_table)
        xs = _out_proj(ret3.reshape(bd, GROUP_WIDTH), fox3.reshape(bd, GROUP_WIDTH), w_out, l, xs, mod, bd, GROUP_WIDTH)
        xs = _channel(xs, mod, l, bd, p, grouped=False)
        outs[4].append(fk32.reshape(bd, 1, N_HEADS, HEAD_DIM))
        outs[5].append(fv32.reshape(bd, 1, N_HEADS, HEAD_DIM))
        outs[6].append(logf.reshape(bd, 1, -1))
        outs[7].append(s_new)
    return (xp.reshape(batch, seq, d), xs.reshape(bd, 1, d)) + tuple(jnp.stack(o) for o in outs)
```

```python
import functools

import jax
import jax.numpy as jnp
from jax import lax
from jax.experimental import pallas as pl
from jax.experimental.pallas import tpu as pltpu

f32, bf16 = jnp.float32, jnp.bfloat16

HEAD_DIM = 128
N_HEADS = 8
GROUP_WIDTH = N_HEADS * HEAD_DIM
RET_CHUNK = 128
PAGE_SIZE = 128
ROPE_BASE = 10000.0
N_EXPERTS = 16
EXPERTS_PER_GROUP = 4
EPS = 1e-6
LANES = 128
SUBLANES = 8
V7X_VMEM_LIMIT = 56 * 1024 * 1024
NEG_BIG = -1e30

_NT = (((1,), (1,)), ((), ()))
_TN = (((0,), (0,)), ((), ()))


def _params(n_axes):
    return pltpu.CompilerParams(dimension_semantics=("arbitrary",) * n_axes, vmem_limit_bytes=V7X_VMEM_LIMIT)


def _silu(x):
    return x * jax.nn.sigmoid(x)


def _log_sigmoid(x):
    return jnp.minimum(x, 0.0) - jnp.log1p(jnp.exp(-jnp.abs(x)))


def _adaln_body(c_ref, w_ref, b_ref, o_ref):
    o_ref[...] = jnp.dot(_silu(c_ref[...]), w_ref[...], preferred_element_type=f32) + b_ref[...]


def _adaln(c_all, w_ada, b_ada):
    depth, d, m6 = w_ada.shape
    nmod, r, tn = m6 // d, c_all.shape[0], 512
    nj = d // tn
    return pl.pallas_call(
        _adaln_body, grid=(depth, nmod, nj),
        in_specs=[pl.BlockSpec((r, d), lambda l, c, j: (0, 0)),
                  pl.BlockSpec((None, d, tn), lambda l, c, j: (l, 0, c * nj + j)),
                  pl.BlockSpec((None, 1, tn), lambda l, c, j: (l, 0, c * nj + j))],
        out_specs=pl.BlockSpec((None, None, r, tn), lambda l, c, j: (l, c, 0, j)),
        out_shape=jax.ShapeDtypeStruct((depth, nmod, r, d), f32),
        compiler_params=_params(3), name="adaln")(c_all, w_ada, b_ada.reshape(depth, 1, m6))


class _Mod:
    def __init__(self, mod, layer, n_sample, rows_per_seq):
        self.mod, self.layer, self.n_sample, self.rows_per_seq = mod, layer, n_sample, rows_per_seq

    def operand(self, chunk, tm, tn, row_of, col_of):
        depth, nmod, r, d = self.mod.shape
        l, bd, rps = self.layer, self.n_sample, self.rows_per_seq
        if rps is None:
            assert tm == bd
            return self.mod, pl.BlockSpec((None, None, tm, tn), lambda *g: (l, chunk, 0, col_of(*g)))
        return (self.mod.reshape(depth, nmod, r, 1, d),
                pl.BlockSpec((None, None, None, 1, tn), lambda *g: (l, chunk, bd + (row_of(*g) * tm) // rps, 0, col_of(*g))))


def _modulated(x_ref, g_ref, sc_ref, sh_ref):
    x = x_ref[...]
    y = x * lax.rsqrt(jnp.mean(x * x, axis=-1, keepdims=True) + EPS) * g_ref[...]
    return y * (1.0 + sc_ref[...]) + sh_ref[...]


def _modulate_body(x_ref, g_ref, sc_ref, sh_ref, o_ref):
    o_ref[...] = _modulated(x_ref, g_ref, sc_ref, sh_ref).astype(o_ref.dtype)


def _route(logits, b_router):
    scores = jax.nn.sigmoid(logits)
    biased = scores + b_router
    lane_i = lax.broadcasted_iota(jnp.int32, logits.shape, 1)
    lane = lane_i.astype(f32)
    grp = lane_i >> 2
    neg = -jnp.inf

    def top2(v):
        t1 = jnp.max(v, axis=-1, keepdims=True)
        i1 = jnp.min(jnp.where(v == t1, lane, float(LANES)), axis=-1, keepdims=True)
        v2 = jnp.where(lane == i1, neg, v)
        t2 = jnp.max(v2, axis=-1, keepdims=True)
        i2 = jnp.min(jnp.where(v2 == t2, lane, float(LANES)), axis=-1, keepdims=True)
        return t1, i1, t2, i2

    best, best_g = None, None
    for g in range(N_EXPERTS // EXPERTS_PER_GROUP):
        t1, _, t2, _ = top2(jnp.where(grp == g, biased, neg))
        gs = t1 + t2
        if best is None:
            best, best_g = gs, jnp.zeros(gs.shape, jnp.int32)
        else:
            take = gs > best
            best_g = jnp.where(take, g, best_g)
            best = jnp.where(take, gs, best)
    _, i1, _, i2 = top2(jnp.where(grp == best_g, biased, neg))
    s1 = jnp.sum(jnp.where(lane == i1, scores, 0.0), axis=-1, keepdims=True)
    s2 = jnp.sum(jnp.where(lane == i2, scores, 0.0), axis=-1, keepdims=True)
    den = s1 + s2
    return jnp.where(lane == i1, s1 / den, 0.0) + jnp.where(lane == i2, s2 / den, 0.0), best_g


def _modulate_route_body(x_ref, g_ref, sc_ref, sh_ref, wr_ref, br_ref, o_ref, comb_ref):
    h = _modulated(x_ref, g_ref, sc_ref, sh_ref)
    d = h.shape[1]
    o_ref[:, :d] = h.astype(o_ref.dtype)
    logits = jnp.dot(h, wr_ref[...], precision=lax.Precision.HIGHEST, preferred_element_type=f32)
    comb, group = _route(logits, br_ref[...])
    lane = lax.broadcasted_iota(jnp.int32, comb.shape, 1)
    comb = jnp.where(lane == N_EXPERTS, group.astype(f32), comb)
    comb_ref[...] = comb
    if o_ref.shape[1] > d:
        o_ref[:, d:] = comb


def _modulate(x, norm_g, mod, sc_chunk, sh_chunk, tm, router=None, pack_weights=False):
    n, d = x.shape
    row, col = (lambda i: i), (lambda i: 0)
    sc, sc_spec = mod.operand(sc_chunk, tm, d, row, col)
    sh, sh_spec = mod.operand(sh_chunk, tm, d, row, col)
    in_specs = [pl.BlockSpec((tm, d), lambda i: (i, 0)), pl.BlockSpec((1, d), lambda i: (0, 0)), sc_spec, sh_spec]
    args = [x, norm_g.reshape(1, d), sc, sh]
    out_specs = pl.BlockSpec((tm, d), lambda i: (i, 0))
    out_shape = jax.ShapeDtypeStruct((n, d), bf16)
    body = _modulate_body
    if router is not None:
        w_router_pad, b_router_pad = router
        in_specs += [pl.BlockSpec((d, LANES), lambda i: (0, 0)), pl.BlockSpec((1, LANES), lambda i: (0, 0))]
        args += [w_router_pad, b_router_pad]
        width = d + LANES if pack_weights else d
        out_specs = [pl.BlockSpec((tm, width), lambda i: (i, 0)), pl.BlockSpec((tm, LANES), lambda i: (i, 0))]
        out_shape = [jax.ShapeDtypeStruct((n, width), f32 if pack_weights else bf16),
                     jax.ShapeDtypeStruct((n, LANES), f32)]
        body = _modulate_route_body
    return pl.pallas_call(body, grid=(n // tm,), in_specs=in_specs, out_specs=out_specs, out_shape=out_shape,
                          compiler_params=_params(1), name="modulate")(*args)


def _proj_body(epilogue, n_extra, h_ref, w_ref, *rest):
    extra, outs, wb_ref = rest[:n_extra], rest[n_extra:-1], rest[-1]

    @pl.when(pl.program_id(1) == 0)
    def _():
        wb_ref[...] = w_ref[...].T.astype(bf16)

    acc = jnp.dot(h_ref[...], wb_ref[...], preferred_element_type=f32)
    epilogue(acc, extra, outs)


def _heads(width):
    return [slice(h * HEAD_DIM, (h + 1) * HEAD_DIM) for h in range(width // HEAD_DIM)]


def _epi_rope(acc, extra, outs):
    cos, sin = extra[0][...], extra[1][...]
    kscale = jnp.where(pl.program_id(0) == 1, HEAD_DIM ** -0.5, 1.0).astype(f32)
    for hs in _heads(acc.shape[1]):
        a = acc[:, hs]
        outs[0][:, hs] = ((a * cos + pltpu.roll(a, HEAD_DIM // 2, 1) * sin) * kscale).astype(outs[0].dtype)


def _epi_headnorm(acc, extra, outs):
    g, post = extra[0][...], extra[1][...]
    for hs in _heads(acc.shape[1]):
        a = acc[:, hs]
        y = a * lax.rsqrt(jnp.mean(a * a, axis=-1, keepdims=True) + EPS) * g
        for o in outs[:-1]:
            o[:, hs] = y.astype(o.dtype)
        outs[-1][:, hs] = (y * post).astype(outs[-1].dtype)


def _epi_plain(acc, extra, outs):
    for o in outs:
        o[...] = acc.astype(o.dtype)


def _epi_forget(acc, extra, outs):
    y = _log_sigmoid(acc + extra[0][...])
    outs[0][...] = y[:, :outs[0].shape[1]]


def _proj(h16, w_t, layer, col0, ncol, tn, tm, epilogue, extra, out_dtypes, out_width=None):
    n, k = h16.shape
    out_width = tn if out_width is None else out_width
    in_specs = [pl.BlockSpec((tm, k), lambda j, i: (i, 0)),
                pl.BlockSpec((None, tn, k), lambda j, i: (layer, col0 + j, 0))] + [s for _, s in extra]
    out_specs = [pl.BlockSpec((tm, out_width), lambda j, i: (i, j)) for _ in out_dtypes]
    out_shape = [jax.ShapeDtypeStruct((n, ncol * out_width), dt) for dt in out_dtypes]
    return pl.pallas_call(
        functools.partial(_proj_body, epilogue, len(extra)), grid=(ncol, n // tm),
        in_specs=in_specs, out_specs=out_specs, out_shape=out_shape,
        scratch_shapes=[pltpu.VMEM((k, tn), bf16)],
        compiler_params=_params(2), name="proj_" + epilogue.__name__[5:])(h16, w_t, *[a for a, _ in extra])


def _group_norm_gate(o, gain, gate):
    mu = jnp.mean(o, axis=-1, keepdims=True)
    var = jnp.mean(jnp.square(o - mu), axis=-1, keepdims=True)
    return (o - mu) * lax.rsqrt(var + EPS) * gain * _silu(gate)


def _ret_prompt_body(q_ref, k_ref, v_ref, g_ref, intra_ref, crossd_ref, kdec_ref, sdec_ref, gn_ref, o_ref, st_ref):
    @pl.when(pl.program_id(1) == 0)
    def _():
        st_ref[...] = jnp.zeros(st_ref.shape, f32)

    for h, hs in enumerate(_heads(q_ref.shape[1])):
        q, k, v = q_ref[:, hs], k_ref[:, hs], v_ref[:, hs]
        state = st_ref[h]
        scores = lax.dot_general(q, k, _NT, preferred_element_type=f32) * intra_ref[h]
        o = jnp.dot(scores.astype(bf16), v, preferred_element_type=f32)
        o = o + jnp.dot(q, state.astype(bf16), preferred_element_type=f32) * crossd_ref[h]
        k_dec = (k.astype(f32) * kdec_ref[h]).astype(bf16)
        st_ref[h] = state * sdec_ref[h] + lax.dot_general(k_dec, v, _TN, preferred_element_type=f32)
        o_ref[:, hs] = _group_norm_gate(o, gn_ref[:, hs], g_ref[:, hs].astype(f32)).astype(o_ref.dtype)


def _ret_tables():
    lg = jnp.log1p(-jnp.exp2(-5.0 - jnp.arange(N_HEADS, dtype=f32)))[:, None, None]
    t = jnp.arange(RET_CHUNK, dtype=f32)
    diff = t[None, :, None] - t[None, None, :]
    intra = jnp.where(diff >= 0, jnp.exp(lg * jnp.maximum(diff, 0.0)), 0.0)
    ones = jnp.ones((1, 1, HEAD_DIM), f32)
    crossd = jnp.exp(lg * (t[None, :, None] + 1.0)) * ones
    kdec = jnp.exp(lg * (RET_CHUNK - 1.0 - t[None, :, None])) * ones
    sdec = jnp.exp(lg * RET_CHUNK) * ones
    step_dec = jnp.exp(lg) * ones
    return intra, crossd, kdec, sdec, step_dec


def _ret_prompt(qk16, vg16, ret_norm_g, batch, tables):
    n = qk16.shape[0]
    nc = n // batch // RET_CHUNK
    intra, crossd, kdec, sdec, _ = tables
    row = lambda b, c: b * nc + c
    full3 = lambda a: pl.BlockSpec(a.shape, lambda b, c: (0, 0, 0))
    blk = lambda col: pl.BlockSpec((RET_CHUNK, GROUP_WIDTH), lambda b, c: (row(b, c), col))
    return pl.pallas_call(
        _ret_prompt_body, grid=(batch, nc),
        in_specs=[blk(0), blk(1), blk(0), blk(1), full3(intra), full3(crossd), full3(kdec), full3(sdec),
                  pl.BlockSpec((1, GROUP_WIDTH), lambda b, c: (0, 0))],
        out_specs=[blk(0), pl.BlockSpec((None, N_HEADS, HEAD_DIM, HEAD_DIM), lambda b, c: (b, 0, 0, 0))],
        out_shape=[jax.ShapeDtypeStruct((n, GROUP_WIDTH), bf16),
                   jax.ShapeDtypeStruct((batch, N_HEADS, HEAD_DIM, HEAD_DIM), f32)],
        compiler_params=_params(2), name="ret_prompt")(
            qk16, qk16, vg16, vg16, intra, crossd, kdec, sdec, ret_norm_g.reshape(1, GROUP_WIDTH))


def _block_diag(x):
    xt = jnp.tile(x, (1, N_HEADS))
    lane = lax.broadcasted_iota(jnp.int32, xt.shape, 1)
    row = lax.broadcasted_iota(jnp.int32, xt.shape, 0)
    return jnp.where((lane >> 7) == row, xt, 0.0)


RET_SAMPLE_SEQS = 4


def _ret_sample_body(q_ref, k_ref, v_ref, g_ref, st_ref, dec3_ref, dec2_ref, gn_ref, o_ref, sn_ref):
    for i in range(q_ref.shape[0]):
        q16, k16, v16 = q_ref[i].astype(bf16), k_ref[i].astype(bf16), v_ref[i].astype(bf16)
        q, k, v = q16.astype(f32), k16.astype(f32), v16.astype(f32)
        state = st_ref[i]
        cross = jnp.dot(_block_diag(q).astype(bf16), state.reshape(GROUP_WIDTH, HEAD_DIM).astype(bf16),
                        preferred_element_type=f32) * dec2_ref[...]
        qk = jnp.sum(q * k, axis=-1, keepdims=True).astype(bf16).astype(f32)
        o = qk * v + cross
        upd = lax.dot_general(_block_diag(k).astype(bf16), v16, _TN, preferred_element_type=f32)
        sn_ref[i] = state * dec3_ref[...] + upd.reshape(state.shape)
        o_ref[i] = _group_norm_gate(o, gn_ref[...], g_ref[i]).astype(o_ref.dtype)


def _ret_sample(q3, k3, v3, g3, state_ret, layer, ret_norm_g, step_dec):
    bd = q3.shape[0]
    nb = RET_SAMPLE_SEQS if bd % RET_SAMPLE_SEQS == 0 else 1
    tok = pl.BlockSpec((nb, N_HEADS, HEAD_DIM), lambda b: (b, 0, 0))
    return pl.pallas_call(
        _ret_sample_body, grid=(bd // nb,),
        in_specs=[tok, tok, tok, tok,
                  pl.BlockSpec((None, nb, N_HEADS, HEAD_DIM, HEAD_DIM), lambda b: (layer, b, 0, 0, 0)),
                  pl.BlockSpec(step_dec.shape, lambda b: (0, 0, 0)),
                  pl.BlockSpec((N_HEADS, HEAD_DIM), lambda b: (0, 0)),
                  pl.BlockSpec((N_HEADS, HEAD_DIM), lambda b: (0, 0))],
        out_specs=[tok, pl.BlockSpec((nb, N_HEADS, HEAD_DIM, HEAD_DIM), lambda b: (b, 0, 0, 0))],
        out_shape=[jax.ShapeDtypeStruct((bd, N_HEADS, HEAD_DIM), f32),
                   jax.ShapeDtypeStruct((bd, N_HEADS, HEAD_DIM, HEAD_DIM), f32)],
        compiler_params=_params(1), name="ret_sample")(
            q3, k3, v3, g3, state_ret, step_dec, step_dec.reshape(N_HEADS, HEAD_DIM),
            ret_norm_g.reshape(N_HEADS, HEAD_DIM))


def _cumsum_body(x_ref, o_ref, carry_ref):
    @pl.when(pl.program_id(1) == 0)
    def _():
        carry_ref[...] = jnp.zeros(carry_ref.shape, f32)

    tb = x_ref.shape[1]
    upper = (lax.broadcasted_iota(jnp.int32, (tb, tb), 0) <= lax.broadcasted_iota(jnp.int32, (tb, tb), 1)).astype(f32)
    cum = jnp.dot(x_ref[...], upper, precision=lax.Precision.HIGHEST, preferred_element_type=f32) + carry_ref[:, :1]
    o_ref[...] = cum
    carry_ref[...] = jnp.broadcast_to(cum[:, tb - 1:tb], carry_ref.shape)


def _cumsum_lanes(x_t, tb):
    b, h, s = x_t.shape
    spec = pl.BlockSpec((None, h, tb), lambda i, j: (i, 0, j))
    return pl.pallas_call(_cumsum_body, grid=(b, s // tb), in_specs=[spec], out_specs=spec,
                          out_shape=jax.ShapeDtypeStruct(x_t.shape, f32),
                          scratch_shapes=[pltpu.VMEM((h, LANES), f32)],
                          compiler_params=_params(2), name="forget_cumsum")(x_t)


LOG2E = 1.4426950408889634
HEADS_PER_STEP = 2


def _forget_tail(f_sel, first_lane, sign, ones_lane):
    x = f_sel * (sign * LOG2E)
    hi = x.astype(bf16).astype(f32)
    rest = x - hi
    mid = rest.astype(bf16).astype(f32)
    lo = (rest - mid).astype(bf16).astype(f32)
    lane = lax.broadcasted_iota(jnp.int32, (f_sel.shape[0], HEAD_DIM), 1)
    ones = jnp.where((lane >= ones_lane) & (lane < ones_lane + 3), 1.0, 0.0)
    tail = jnp.where(lane == first_lane, hi, jnp.where(lane == first_lane + 1, mid,
                                                      jnp.where(lane == first_lane + 2, lo, ones)))
    return tail.astype(bf16)


def _select_head(f_block, head):
    lane = lax.broadcasted_iota(jnp.int32, f_block.shape, 1)
    return jnp.sum(jnp.where(lane == head, f_block, 0.0), axis=-1, keepdims=True)


def _fox_prompt_body(q_ref, k_ref, v_ref, fq_ref, fk_ref, o_ref, qaug_ref, kaug_ref, m_ref, l_ref, acc_ref):
    pair, qi = pl.program_id(1), pl.program_id(2)
    tq = q_ref.shape[0]
    n_sub = q_ref.shape[1] // HEAD_DIM
    seq = k_ref.shape[0]

    @pl.when(qi == 0)
    def _():
        def fill(c, carry):
            rows = pl.ds(pl.multiple_of(c * tq, tq), tq)
            for hh, hs in enumerate(_heads(q_ref.shape[1])):
                kaug_ref[hh, rows, :HEAD_DIM] = k_ref[rows, hs]
                kaug_ref[hh, rows, HEAD_DIM:] = _forget_tail(_select_head(fk_ref[rows, :], pair * n_sub + hh), 3, -1.0, 0)
            return carry
        lax.fori_loop(0, seq // tq, fill, 0)

    for hh, hs in enumerate(_heads(q_ref.shape[1])):
        qaug_ref[hh, :, :HEAD_DIM] = q_ref[:, hs]
        qaug_ref[hh, :, HEAD_DIM:] = _forget_tail(_select_head(fq_ref[...], pair * n_sub + hh), 0, 1.0, 3)
    m_ref[...] = jnp.full(m_ref.shape, NEG_BIG, f32)
    l_ref[...] = jnp.zeros(l_ref.shape, f32)
    acc_ref[...] = jnp.zeros(acc_ref.shape, f32)

    def step(j, diagonal):
        rows = pl.ds(pl.multiple_of(j * tq, tq), tq)
        for hh, hs in enumerate(_heads(q_ref.shape[1])):
            s = lax.dot_general(qaug_ref[hh], kaug_ref[hh, rows, :], _NT, preferred_element_type=f32)
            if diagonal:
                r = lax.broadcasted_iota(jnp.int32, s.shape, 0)
                c = lax.broadcasted_iota(jnp.int32, s.shape, 1)
                s = jnp.where(c <= r, s, -jnp.inf)
            m_old = m_ref[hh]
            m_new = jnp.maximum(m_old, jnp.max(s, axis=-1, keepdims=True))
            alpha = jnp.exp2(m_old - m_new)
            p = jnp.exp2(s - jnp.tile(m_new, (1, tq // LANES)))
            l_ref[hh] = alpha * l_ref[hh] + jnp.sum(p, axis=-1, keepdims=True)
            acc_ref[hh] = alpha * acc_ref[hh] + jnp.dot(p.astype(bf16), v_ref[rows, hs], preferred_element_type=f32)
            m_ref[hh] = m_new

    def off_diagonal(j, carry):
        step(j, False)
        return carry

    lax.fori_loop(0, qi, off_diagonal, 0)
    step(qi, True)
    for hh, hs in enumerate(_heads(q_ref.shape[1])):
        o_ref[:, hs] = (acc_ref[hh] / l_ref[hh]).astype(o_ref.dtype)


def _fox_prompt(q16, k16, v16, f_cols, batch, tq):
    n = q16.shape[0]
    s = n // batch
    nq = s // tq
    width = HEADS_PER_STEP * HEAD_DIM
    kv = lambda a: a.reshape(batch, s, GROUP_WIDTH)
    kv_spec = pl.BlockSpec((None, s, width), lambda b, h, i: (b, 0, h))
    q_spec = pl.BlockSpec((tq, width), lambda b, h, i: (b * nq + i, h))
    sub = (HEADS_PER_STEP, tq, HEAD_DIM)
    return pl.pallas_call(
        _fox_prompt_body, grid=(batch, N_HEADS // HEADS_PER_STEP, nq),
        in_specs=[q_spec, kv_spec, kv_spec,
                  pl.BlockSpec((None, tq, N_HEADS), lambda b, h, i: (b, i, 0)),
                  pl.BlockSpec((None, s, N_HEADS), lambda b, h, i: (b, 0, 0))],
        out_specs=q_spec,
        out_shape=jax.ShapeDtypeStruct((n, GROUP_WIDTH), bf16),
        scratch_shapes=[pltpu.VMEM((HEADS_PER_STEP, tq, 2 * HEAD_DIM), bf16),
                        pltpu.VMEM((HEADS_PER_STEP, s, 2 * HEAD_DIM), bf16),
                        pltpu.VMEM(sub, f32), pltpu.VMEM(sub, f32), pltpu.VMEM(sub, f32)],
        compiler_params=_params(3), name="fox_prompt")(q16, kv(k16), kv(v16), f_cols, f_cols)


FLAT = PAGE_SIZE * N_HEADS
PAGES_PER_STEP = 8


def _dot_select(y, m01):
    hi = y.astype(bf16)
    rest = y - hi.astype(f32)
    mid = rest.astype(bf16)
    lo = (rest - mid.astype(f32)).astype(bf16)
    return sum(jnp.dot(piece, m01, preferred_element_type=f32) for piece in (hi, mid, lo))


def _forget_pages_body(x_ref, excl_ref, tot_ref):
    x = x_ref[...]
    t = x.shape[1]
    later = (lax.broadcasted_iota(jnp.int32, (t, t), 0) > lax.broadcasted_iota(jnp.int32, (t, t), 1)).astype(bf16)
    excl_ref[...] = _dot_select(x, later)
    tot_ref[...] = jnp.broadcast_to(jnp.sum(x, axis=-1, keepdims=True), x.shape)


def _forget_pages(cache_logf):
    depth, n_phys, page, heads = cache_logf.shape
    assert page == PAGE_SIZE and heads == N_HEADS
    n = depth * n_phys
    assert n % SUBLANES == 0
    nr = n * heads
    tr = next((c for c in (2048, 1024, 512, 256, 128, 64) if nr % c == 0), nr)
    rows = cache_logf.transpose(0, 1, 3, 2).reshape(nr, page)
    spec = pl.BlockSpec((tr, page), lambda i: (i, 0))
    out = jax.ShapeDtypeStruct((nr, page), f32)
    excl, tot = pl.pallas_call(_forget_pages_body, grid=(nr // tr,), in_specs=[spec], out_specs=[spec, spec],
                               out_shape=[out, out], compiler_params=_params(1), name="forget_pages")(rows)
    excl_flat = excl.reshape(n, heads, page).transpose(0, 2, 1).reshape(n, FLAT)
    tot_flat = jnp.tile(tot[:, 0].reshape(n, heads), (1, page))
    return excl_flat, tot_flat


def _fox_sample_body(n_group, page_row0, pt_ref, q_ref, kn_ref, vn_ref, lnew_ref, *rest):
    k_refs, v_refs = rest[:n_group], rest[n_group:2 * n_group]
    excl_refs, tot_refs = rest[2 * n_group:3 * n_group], rest[3 * n_group:4 * n_group]
    o_ref, carry_ref, m_ref, l_ref, acc_ref = rest[4 * n_group:]
    step = pl.program_id(1)
    n_pages = pl.num_programs(1) * n_group
    scale = HEAD_DIM ** -0.5

    @pl.when(step == 0)
    def _():
        carry_ref[...] = lnew_ref[...]
        m_ref[...] = jnp.full(m_ref.shape, NEG_BIG, f32)
        l_ref[...] = jnp.zeros(l_ref.shape, f32)
        acc_ref[...] = jnp.zeros(acc_ref.shape, f32)

    q16 = q_ref[...].astype(bf16)
    own_head = ((lax.broadcasted_iota(jnp.int32, (N_HEADS, FLAT), 1) & (N_HEADS - 1))
                == lax.broadcasted_iota(jnp.int32, (N_HEADS, FLAT), 0))
    carry = carry_ref[...]
    scores = []
    for j in range(n_group):
        kf = k_refs[j][...].reshape(FLAT, HEAD_DIM).astype(bf16)
        row = pl.ds((page_row0 + pt_ref[pl.program_id(0), n_pages - 1 - (step * n_group + j)]) % SUBLANES, 1)
        s = lax.dot_general(q16, kf, _NT, preferred_element_type=f32) * scale + (carry + excl_refs[j][row, :])
        scores.append(jnp.where(own_head, s, -jnp.inf))
        carry = carry + tot_refs[j][row, :]
    carry_ref[...] = carry

    top = scores[0]
    for s in scores[1:]:
        top = jnp.maximum(top, s)
    m_old = m_ref[...]
    m_new = jnp.maximum(m_old, jnp.max(top, axis=-1, keepdims=True))
    alpha = jnp.exp(m_old - m_new)
    l_new = alpha * l_ref[...]
    acc = alpha * acc_ref[...]
    for j in range(n_group):
        pe = jnp.exp(scores[j] - m_new)
        l_new = l_new + jnp.sum(pe, axis=-1, keepdims=True)
        vf = v_refs[j][...].reshape(FLAT, HEAD_DIM).astype(bf16)
        acc = acc + jnp.dot(pe.astype(bf16), vf, preferred_element_type=f32)
    m_ref[...] = m_new
    l_ref[...] = l_new
    acc_ref[...] = acc

    @pl.when(step == pl.num_programs(1) - 1)
    def _():
        rounded = lambda ref: ref[...].astype(bf16).astype(f32)
        s_new = jnp.sum(rounded(q_ref) * rounded(kn_ref), axis=-1, keepdims=True) * scale
        m_fin = jnp.maximum(m_new, s_new)
        beta = jnp.exp(m_new - m_fin)
        p_new = jnp.exp(s_new - m_fin)
        num = beta * acc + p_new.astype(bf16).astype(f32) * rounded(vn_ref)
        o_ref[...] = (num / (beta * l_new + p_new)).astype(o_ref.dtype)


def _fox_sample(q3, kn3, vn3, logf_new, cache_k, cache_v, excl_flat, tot_flat, layer, page_table):
    bd, n_pages = page_table.shape
    n_phys = cache_k.shape[1]
    g = PAGES_PER_STEP if n_pages % PAGES_PER_STEP == 0 else 1
    lnew_flat = jnp.tile(logf_new, (1, PAGE_SIZE)).reshape(bd, 1, FLAT)
    tok = pl.BlockSpec((None, N_HEADS, HEAD_DIM), lambda b, s, pt: (b, 0, 0))

    def page_of(j):
        return lambda b, s, pt: pt[b, n_pages - 1 - (s * g + j)]

    def kv_spec(j):
        pg = page_of(j)
        return pl.BlockSpec((None, None, PAGE_SIZE, N_HEADS, HEAD_DIM), lambda b, s, pt: (layer, pg(b, s, pt), 0, 0, 0))

    def flat_spec(j):
        pg = page_of(j)
        return pl.BlockSpec((SUBLANES, FLAT), lambda b, s, pt: ((layer * n_phys + pg(b, s, pt)) // SUBLANES, 0))

    groups = [[kv_spec(j) for j in range(g)], [kv_spec(j) for j in range(g)],
              [flat_spec(j) for j in range(g)], [flat_spec(j) for j in range(g)]]
    grid_spec = pltpu.PrefetchScalarGridSpec(
        num_scalar_prefetch=1, grid=(bd, n_pages // g),
        in_specs=[tok, tok, tok, pl.BlockSpec((None, 1, FLAT), lambda b, s, pt: (b, 0, 0))] + sum(groups, []),
        out_specs=tok,
        scratch_shapes=[pltpu.VMEM((1, FLAT), f32), pltpu.VMEM((N_HEADS, 1), f32), pltpu.VMEM((N_HEADS, 1), f32),
                        pltpu.VMEM((N_HEADS, HEAD_DIM), f32)])
    return pl.pallas_call(
        functools.partial(_fox_sample_body, g, layer * n_phys), grid_spec=grid_spec,
        out_shape=jax.ShapeDtypeStruct((bd, N_HEADS, HEAD_DIM), f32),
        compiler_params=_params(2), name="fox_sample")(
            page_table, q3, kn3, vn3, lnew_flat, *([cache_k] * g + [cache_v] * g + [excl_flat] * g + [tot_flat] * g))


def _out_proj_body(ret_ref, fox_ref, w_ref, x_ref, ga_ref, o_ref, wb_ref):
    @pl.when(pl.program_id(1) == 0)
    def _():
        wb_ref[...] = w_ref[...].astype(bf16)

    half = ret_ref.shape[1]
    y = jnp.dot(ret_ref[...].astype(bf16), wb_ref[:half, :], preferred_element_type=f32)
    y = y + jnp.dot(fox_ref[...].astype(bf16), wb_ref[half:, :], preferred_element_type=f32)
    o_ref[...] = x_ref[...] + ga_ref[...] * y


def _out_proj(ret16, fox16, w_out, layer, x, mod, tm, tn):
    n, d = x.shape
    k = w_out.shape[1]
    ga, ga_spec = mod.operand(2, tm, tn, lambda j, i: i, lambda j, i: j)
    act = pl.BlockSpec((tm, ret16.shape[1]), lambda j, i: (i, 0))
    xs = pl.BlockSpec((tm, tn), lambda j, i: (i, j))
    return pl.pallas_call(
        _out_proj_body, grid=(d // tn, n // tm),
        in_specs=[act, act, pl.BlockSpec((None, k, tn), lambda j, i: (layer, 0, j)), xs, ga_spec],
        out_specs=xs, out_shape=jax.ShapeDtypeStruct((n, d), f32),
        scratch_shapes=[pltpu.VMEM((k, tn), bf16)],
        compiler_params=_params(2), name="out_proj")(ret16, fox16, w_out, x, ga)


def _expert(h, wg_ref, wu_ref, wd_ref):
    g = jnp.dot(h, wg_ref[...], preferred_element_type=f32)
    u = jnp.dot(h, wu_ref[...], preferred_element_type=f32)
    return jnp.dot(_silu(g) * u, wd_ref[...], preferred_element_type=f32)


def _moe_body(h_ref, comb_ref, wg_ref, wu_ref, wd_ref, x_ref, gm_ref, o_ref, acc_ref):
    e = pl.program_id(1)

    @pl.when(e == 0)
    def _():
        acc_ref[...] = jnp.zeros(acc_ref.shape, f32)

    y = _expert(h_ref[...].astype(f32), wg_ref, wu_ref, wd_ref)
    lane = lax.broadcasted_iota(jnp.int32, comb_ref.shape, 1)
    w = jnp.sum(jnp.where(lane == e, comb_ref[...], 0.0), axis=-1, keepdims=True)
    acc_ref[...] += w * y

    @pl.when(e == pl.num_programs(1) - 1)
    def _():
        o_ref[...] = x_ref[...] + gm_ref[...] * acc_ref[...]


def _moe(h16, comb, wg, wu, wd, layer, x, mod, tm):
    n, d = x.shape
    n_exp, _, de = wg.shape[1:]
    gm, gm_spec = mod.operand(5, tm, d, lambda i, e: i, lambda i, e: 0)
    rows = lambda w: pl.BlockSpec((tm, w), lambda i, e: (i, 0))
    return pl.pallas_call(
        _moe_body, grid=(n // tm, n_exp),
        in_specs=[rows(d), rows(LANES),
                  pl.BlockSpec((None, None, d, de), lambda i, e: (layer, e, 0, 0)),
                  pl.BlockSpec((None, None, d, de), lambda i, e: (layer, e, 0, 0)),
                  pl.BlockSpec((None, None, de, d), lambda i, e: (layer, e, 0, 0)),
                  rows(d), gm_spec],
        out_specs=rows(d), out_shape=jax.ShapeDtypeStruct((n, d), f32),
        scratch_shapes=[pltpu.VMEM((tm, d), f32)],
        compiler_params=_params(2), name="moe")(h16, comb, wg, wu, wd, x, gm)


N_GROUPS = N_EXPERTS // EXPERTS_PER_GROUP
ROW_DMA_UNROLL = 8


def _group_layout(group, tile):
    n = group.shape[0]
    onehot = (group[:, None] == jnp.arange(N_GROUPS, dtype=jnp.int32)[None, :]).astype(jnp.int32)
    running = jnp.cumsum(onehot, axis=0)
    rank = jnp.sum(onehot * running, axis=1) - 1
    padded = (running[-1] + tile - 1) // tile * tile
    ends = jnp.cumsum(padded)
    slot = jnp.sum(onehot * (ends - padded)[None, :], axis=1) + rank
    starts = jnp.arange(n // tile + N_GROUPS, dtype=jnp.int32) * tile
    tile_group = jnp.minimum(jnp.sum((starts[:, None] >= ends[None, :]).astype(jnp.int32), axis=1), N_GROUPS - 1)
    return slot.astype(jnp.int32), tile_group.astype(jnp.int32), (ends[-1] // tile).astype(jnp.int32).reshape(1)


def _start_rows(n_rows, copy_of):
    def start(c, carry):
        for u in range(ROW_DMA_UNROLL):
            copy_of(c * ROW_DMA_UNROLL + u).start(priority=u % 2)
        return carry

    lax.fori_loop(0, n_rows // ROW_DMA_UNROLL, start, 0)


def _wait_rows(n_rows, copy_of):
    def wait(c, carry):
        for u in range(ROW_DMA_UNROLL):
            copy_of(c * ROW_DMA_UNROLL + u).wait()
        return carry

    lax.fori_loop(0, n_rows // ROW_DMA_UNROLL, wait, 0)


def _snake(tile, step):
    return jnp.where(tile % 2 == 0, step, EXPERTS_PER_GROUP - 1 - step)


def _moe_group_body(tile_rows, slot_ref, tg_ref, used_ref, hx_hbm, wg_ref, wu_ref, wd_ref, o_ref,
                    rows_ref, source_ref, sems):
    tile, e4 = pl.program_id(0), pl.program_id(1)
    d = wg_ref.shape[0]
    used = used_ref[0]

    def gather(t):
        return lambda k: pltpu.make_async_copy(hx_hbm.at[pl.ds(source_ref[t * tile_rows + k], 1)],
                                               rows_ref.at[t % 2, pl.ds(k, 1)], sems.at[t % 2])

    @pl.when((tile == 0) & (e4 == 0))
    def _():
        def clear(c, carry):
            for u in range(ROW_DMA_UNROLL):
                source_ref[c * ROW_DMA_UNROLL + u] = 0
            return carry

        def place(c, carry):
            for u in range(ROW_DMA_UNROLL):
                source_ref[slot_ref[c * ROW_DMA_UNROLL + u]] = c * ROW_DMA_UNROLL + u
            return carry

        lax.fori_loop(0, source_ref.shape[0] // ROW_DMA_UNROLL, clear, 0)
        lax.fori_loop(0, slot_ref.shape[0] // ROW_DMA_UNROLL, place, 0)
        _start_rows(tile_rows, gather(0))

    @pl.when((e4 == 1) & (tile + 1 < used))
    def _():
        _start_rows(tile_rows, gather(tile + 1))

    @pl.when(e4 == 0)
    def _():
        o_ref[...] = jnp.zeros(o_ref.shape, f32)

    @pl.when((e4 == 0) & (tile < used))
    def _():
        _wait_rows(tile_rows, gather(tile))

    @pl.when(tile < used)
    def _():
        rows = rows_ref.at[tile % 2]
        y = _expert(rows[:, :d], wg_ref, wu_ref, wd_ref)
        e = tg_ref[tile] * EXPERTS_PER_GROUP + _snake(tile, e4)
        comb = rows[:, d:]
        lane = lax.broadcasted_iota(jnp.int32, comb.shape, 1)
        o_ref[...] += jnp.sum(jnp.where(lane == e, comb, 0.0), axis=-1, keepdims=True) * y


def _moe_group(hx, slot, tile_group, n_used, wg, wu, wd, layer, tile):
    n_slots = hx.shape[0] + N_GROUPS * tile
    d, de = wg.shape[2:]
    assert hx.shape[0] % ROW_DMA_UNROLL == 0 and tile % ROW_DMA_UNROLL == 0

    def expert(i, e4, sl, tg, used):
        last = used[0] - 1
        return jnp.where(i <= last, tg[i] * EXPERTS_PER_GROUP + _snake(i, e4),
                         tg[last] * EXPERTS_PER_GROUP + _snake(last, EXPERTS_PER_GROUP - 1))

    weights = lambda rows, cols: pl.BlockSpec((None, None, rows, cols),
                                              lambda i, e4, sl, tg, used: (layer, expert(i, e4, sl, tg, used), 0, 0))
    grid_spec = pltpu.PrefetchScalarGridSpec(
        num_scalar_prefetch=3, grid=(n_slots // tile, EXPERTS_PER_GROUP),
        in_specs=[pl.BlockSpec(memory_space=pl.ANY), weights(d, de), weights(d, de), weights(de, d)],
        out_specs=pl.BlockSpec((tile, d), lambda i, e4, sl, tg, used: (i, 0)),
        scratch_shapes=[pltpu.VMEM((2, tile, hx.shape[1]), f32), pltpu.SMEM((n_slots,), jnp.int32),
                        pltpu.SemaphoreType.DMA((2,))])
    return pl.pallas_call(functools.partial(_moe_group_body, tile), grid_spec=grid_spec,
                          out_shape=jax.ShapeDtypeStruct((n_slots, d), f32),
                          compiler_params=_params(2), name="moe_group")(slot, tile_group, n_used, hx, wg, wu, wd)


def _moe_combine_body(slot_ref, x_ref, gm_ref, ys_hbm, o_ref, rows_ref, sem):
    tm = x_ref.shape[0]
    base = pl.program_id(0) * tm
    fetch = lambda k: pltpu.make_async_copy(ys_hbm.at[pl.ds(slot_ref[base + k], 1)], rows_ref.at[pl.ds(k, 1)], sem)
    _start_rows(tm, fetch)
    _wait_rows(tm, fetch)
    o_ref[...] = x_ref[...] + gm_ref[...] * rows_ref[...]


def _moe_combine(x, mod, ys, slot, tm):
    n, d = x.shape
    gm, gm_spec = mod.operand(5, tm, d, lambda i, s: i, lambda i, s: 0)
    rows = pl.BlockSpec((tm, d), lambda i, s: (i, 0))
    grid_spec = pltpu.PrefetchScalarGridSpec(
        num_scalar_prefetch=1, grid=(n // tm,),
        in_specs=[rows, gm_spec, pl.BlockSpec(memory_space=pl.ANY)], out_specs=rows,
        scratch_shapes=[pltpu.VMEM((tm, d), f32), pltpu.SemaphoreType.DMA(())])
    return pl.pallas_call(_moe_combine_body, grid_spec=grid_spec, out_shape=jax.ShapeDtypeStruct((n, d), f32),
                          compiler_params=_params(1), name="moe_combine")(slot, x, gm, ys)


def _rope_tables(positions):
    half = HEAD_DIM // 2
    inv_freq = ROPE_BASE ** (-jnp.arange(half, dtype=f32) / half)
    ang = positions.astype(f32)[:, None] * inv_freq[None, :]
    cos, sin = jnp.cos(ang), jnp.sin(ang)
    return jnp.concatenate([cos, cos], axis=-1), jnp.concatenate([-sin, sin], axis=-1)


def _mixer_in(x, mod, layer, tm, pos_tables, pos_tiles, p, lo, q_scale):
    h16 = _modulate(x, p["norm_mix_g"][layer], mod, 1, 0, tm)
    cos, sin = pos_tables
    pos_spec = pl.BlockSpec((tm, HEAD_DIM), lambda j, i: (i % pos_tiles, 0))
    w_in = p["w_in_t"]
    gw = GROUP_WIDTH
    (qk16,) = _proj(h16, w_in, layer, 0, 2, gw, tm, _epi_rope, [(cos, pos_spec), (sin, pos_spec)], [lo])
    (vg16,) = _proj(h16, w_in, layer, 2, 2, gw, tm, _epi_plain, [], [lo])
    row = lambda a: (a.reshape(1, HEAD_DIM), pl.BlockSpec((1, HEAD_DIM), lambda j, i: (0, 0)))
    const = lambda c: row(jnp.full((HEAD_DIM,), c, f32))
    (fq16,) = _proj(h16, w_in, layer, 4, 1, gw, tm, _epi_headnorm, [row(p["q_norm_g"][layer]), const(q_scale)], [lo])
    fk32, fk16 = _proj(h16, w_in, layer, 5, 1, gw, tm, _epi_headnorm, [row(p["k_norm_g"][layer]), const(1.0)], [f32, lo])
    fv32, fv16 = _proj(h16, w_in, layer, 6, 1, gw, tm, _epi_plain, [], [f32, lo])
    n_forget = p["b_forget"].shape[1]
    w_f = jnp.pad(w_in[layer, 7 * gw:, :], ((0, LANES - n_forget), (0, 0)))[None]
    b_f = jnp.pad(p["b_forget"][layer], (0, LANES - n_forget)).reshape(1, LANES)
    (logf,) = _proj(h16, w_f, 0, 0, 1, LANES, tm, _epi_forget,
                    [(b_f, pl.BlockSpec((1, LANES), lambda j, i: (0, 0)))], [f32], out_width=n_forget)
    return qk16, vg16, fq16, fk32, fk16, fv32, fv16, logf


def _channel(x, mod, layer, tm, p, grouped):
    router = (p["w_router_pad"], p["b_router_pad"])
    experts = (p["w_gate"], p["w_up"], p["w_down"])
    if not grouped:
        h16, comb = _modulate(x, p["norm_ffn_g"][layer], mod, 4, 3, tm, router=router)
        return _moe(h16, comb, *experts, layer, x, mod, tm)
    hx, comb = _modulate(x, p["norm_ffn_g"][layer], mod, 4, 3, tm, router=router, pack_weights=True)
    slot, tile_group, n_used = _group_layout(comb[:, N_EXPERTS].astype(jnp.int32), tm)
    ys = _moe_group(hx, slot, tile_group, n_used, *experts, layer, tm)
    return _moe_combine(x, mod, ys, slot, tm)


def kernel(x_prompt, x_sample, cache_k, cache_v, cache_logf, state_ret, page_table, c_prompt, c_sample, w_ada, b_ada,
           norm_mix_g, norm_ffn_g, w_in, b_forget, q_norm_g, k_norm_g, ret_norm_g, w_out, w_router, b_router,
           w_gate, w_up, w_down):
    batch, seq, d = x_prompt.shape
    bd, t_new, _ = x_sample.shape
    assert t_new == 1, "one new token per sampled sequence"
    assert bd % 8 == 0 and d == 2 * GROUP_WIDTH
    depth = w_in.shape[0]
    n_pages = page_table.shape[1]
    past_len = n_pages * PAGE_SIZE
    n_p = batch * seq
    tm_p = min(512, seq)
    tq = min(512, seq)

    p = dict(norm_mix_g=norm_mix_g, norm_ffn_g=norm_ffn_g, w_in_t=w_in.transpose(0, 2, 1), b_forget=b_forget,
             q_norm_g=q_norm_g, k_norm_g=k_norm_g,
             w_router_pad=jnp.pad(w_router, ((0, 0), (0, LANES - w_router.shape[1]))),
             b_router_pad=jnp.pad(b_router, (0, LANES - b_router.shape[0])).reshape(1, LANES),
             w_gate=w_gate, w_up=w_up, w_down=w_down)

    r_rows = -(-(bd + batch) // 8) * 8
    c_all = jnp.concatenate([c_sample, c_prompt, jnp.zeros((r_rows - bd - batch, d), f32)], axis=0)
    mod_all = _adaln(c_all, w_ada, b_ada)

    excl_flat, tot_flat = _forget_pages(cache_logf)
    tables = _ret_tables()
    rope_p = _rope_tables(jnp.arange(seq))
    rope_s = _rope_tables(jnp.full((bd,), past_len))

    xp = x_prompt.reshape(n_p, d)
    xs = x_sample.reshape(bd, d)
    outs = [[] for _ in range(8)]
    tok3 = lambda a: a.reshape(bd, N_HEADS, HEAD_DIM)
    for l in range(depth):
        mod = _Mod(mod_all, l, bd, seq)
        qk16, vg16, fq16, fk32, fk16, fv32, fv16, logf = _mixer_in(xp, mod, l, tm_p, rope_p, seq // tm_p, p, bf16,
                                                                       LOG2E * HEAD_DIM ** -0.5)
        ret16, s_fin = _ret_prompt(qk16, vg16, ret_norm_g[l], batch, tables)
        logf_b = logf.reshape(batch, seq, -1)
        f_rows = _cumsum_lanes(logf_b.transpose(0, 2, 1), tq)
        fox16 = _fox_prompt(fq16, fk16, fv16, f_rows.transpose(0, 2, 1), batch, tq)
        xp = _out_proj(ret16, fox16, w_out, l, xp, mod, tm_p, GROUP_WIDTH)
        xp = _channel(xp, mod, l, tm_p, p, grouped=True)
        outs[0].append(fk32.reshape(batch, seq, N_HEADS, HEAD_DIM))
        outs[1].append(fv32.reshape(batch, seq, N_HEADS, HEAD_DIM))
        outs[2].append(logf_b)
        outs[3].append(s_fin)
        mod = _Mod(mod_all, l, bd, None)
        qk16, vg16, fq16, fk32, fk16, fv32, fv16, logf = _mixer_in(xs, mod, l, bd, rope_s, 1, p, f32, 1.0)
        ret3, s_new = _ret_sample(tok3(qk16[:, :GROUP_WIDTH]), tok3(qk16[:, GROUP_WIDTH:]), tok3(vg16[:, :GROUP_WIDTH]),
                                  tok3(vg16[:, GROUP_WIDTH:]), state_ret, l, ret_norm_g[l], tables[4])
        fox3 = _fox_sample(tok3(fq16), tok3(fk16), tok3(fv16), logf, cache_k, cache_v, excl_flat, tot_flat, l, page_table)
        xs = _out_proj(ret3.reshape(bd, GROUP_WIDTH), fox3.reshape(bd, GROUP_WIDTH), w_out, l, xs, mod, bd, GROUP_WIDTH)
        xs = _channel(xs, mod, l, bd, p, grouped=False)
        outs[4].append(fk32.reshape(bd, 1, N_HEADS, HEAD_DIM))
        outs[5].append(fv32.reshape(bd, 1, N_HEADS, HEAD_DIM))
        outs[6].append(logf.reshape(bd, 1, -1))
        outs[7].append(s_new)
    return (xp.reshape(batch, seq, d), xs.reshape(bd, 1, d)) + tuple(jnp.stack(o) for o in outs)
```

```python
import functools

import jax
import jax.numpy as jnp
from jax import lax
from jax.experimental import pallas as pl
from jax.experimental.pallas import tpu as pltpu

f32, bf16 = jnp.float32, jnp.bfloat16

HEAD_DIM = 128
N_HEADS = 8
GROUP_WIDTH = N_HEADS * HEAD_DIM
RET_CHUNK = 128
PAGE_SIZE = 128
ROPE_BASE = 10000.0
N_EXPERTS = 16
EXPERTS_PER_GROUP = 4
EPS = 1e-6
LANES = 128
SUBLANES = 8
V7X_VMEM_LIMIT = 56 * 1024 * 1024
NEG_BIG = -1e30

_NT = (((1,), (1,)), ((), ()))
_TN = (((0,), (0,)), ((), ()))


def _params(n_axes):
    return pltpu.CompilerParams(dimension_semantics=("arbitrary",) * n_axes, vmem_limit_bytes=V7X_VMEM_LIMIT)


def _silu(x):
    return x * jax.nn.sigmoid(x)


def _log_sigmoid(x):
    return jnp.minimum(x, 0.0) - jnp.log1p(jnp.exp(-jnp.abs(x)))


def _adaln_body(c_ref, w_ref, b_ref, o_ref):
    o_ref[...] = jnp.dot(_silu(c_ref[...]), w_ref[...], preferred_element_type=f32) + b_ref[...]


def _adaln(c_all, w_ada, b_ada):
    depth, d, m6 = w_ada.shape
    nmod, r, tn = m6 // d, c_all.shape[0], 512
    nj = d // tn
    return pl.pallas_call(
        _adaln_body, grid=(depth, nmod, nj),
        in_specs=[pl.BlockSpec((r, d), lambda l, c, j: (0, 0)),
                  pl.BlockSpec((None, d, tn), lambda l, c, j: (l, 0, c * nj + j)),
                  pl.BlockSpec((None, 1, tn), lambda l, c, j: (l, 0, c * nj + j))],
        out_specs=pl.BlockSpec((None, None, r, tn), lambda l, c, j: (l, c, 0, j)),
        out_shape=jax.ShapeDtypeStruct((depth, nmod, r, d), f32),
        compiler_params=_params(3), name="adaln")(c_all, w_ada, b_ada.reshape(depth, 1, m6))


class _Mod:
    def __init__(self, mod, layer, n_sample, rows_per_seq):
        self.mod, self.layer, self.n_sample, self.rows_per_seq = mod, layer, n_sample, rows_per_seq

    def operand(self, chunk, tm, tn, row_of, col_of):
        depth, nmod, r, d = self.mod.shape
        l, bd, rps = self.layer, self.n_sample, self.rows_per_seq
        if rps is None:
            assert tm == bd
            return self.mod, pl.BlockSpec((None, None, tm, tn), lambda *g: (l, chunk, 0, col_of(*g)))
        return (self.mod.reshape(depth, nmod, r, 1, d),
                pl.BlockSpec((None, None, None, 1, tn), lambda *g: (l, chunk, bd + (row_of(*g) * tm) // rps, 0, col_of(*g))))


def _modulated(x_ref, g_ref, sc_ref, sh_ref):
    x = x_ref[...]
    y = x * lax.rsqrt(jnp.mean(x * x, axis=-1, keepdims=True) + EPS) * g_ref[...]
    return y * (1.0 + sc_ref[...]) + sh_ref[...]


def _modulate_body(x_ref, g_ref, sc_ref, sh_ref, o_ref):
    o_ref[...] = _modulated(x_ref, g_ref, sc_ref, sh_ref).astype(o_ref.dtype)


def _route(logits, b_router):
    scores = jax.nn.sigmoid(logits)
    biased = scores + b_router
    lane_i = lax.broadcasted_iota(jnp.int32, logits.shape, 1)
    lane = lane_i.astype(f32)
    grp = lane_i >> 2
    neg = -jnp.inf

    def top2(v):
        t1 = jnp.max(v, axis=-1, keepdims=True)
        i1 = jnp.min(jnp.where(v == t1, lane, float(LANES)), axis=-1, keepdims=True)
        v2 = jnp.where(lane == i1, neg, v)
        t2 = jnp.max(v2, axis=-1, keepdims=True)
        i2 = jnp.min(jnp.where(v2 == t2, lane, float(LANES)), axis=-1, keepdims=True)
        return t1, i1, t2, i2

    best, best_g = None, None
    for g in range(N_EXPERTS // EXPERTS_PER_GROUP):
        t1, _, t2, _ = top2(jnp.where(grp == g, biased, neg))
        gs = t1 + t2
        if best is None:
            best, best_g = gs, jnp.zeros(gs.shape, jnp.int32)
        else:
            take = gs > best
            best_g = jnp.where(take, g, best_g)
            best = jnp.where(take, gs, best)
    _, i1, _, i2 = top2(jnp.where(grp == best_g, biased, neg))
    s1 = jnp.sum(jnp.where(lane == i1, scores, 0.0), axis=-1, keepdims=True)
    s2 = jnp.sum(jnp.where(lane == i2, scores, 0.0), axis=-1, keepdims=True)
    den = s1 + s2
    return jnp.where(lane == i1, s1 / den, 0.0) + jnp.where(lane == i2, s2 / den, 0.0), best_g


def _modulate_route_body(x_ref, g_ref, sc_ref, sh_ref, wr_ref, br_ref, o_ref, comb_ref):
    h = _modulated(x_ref, g_ref, sc_ref, sh_ref)
    d = h.shape[1]
    o_ref[:, :d] = h.astype(o_ref.dtype)
    logits = jnp.dot(h, wr_ref[...], precision=lax.Precision.HIGHEST, preferred_element_type=f32)
    comb, group = _route(logits, br_ref[...])
    lane = lax.broadcasted_iota(jnp.int32, comb.shape, 1)
    comb = jnp.where(lane == N_EXPERTS, group.astype(f32), comb)
    comb_ref[...] = comb
    if o_ref.shape[1] > d:
        o_ref[:, d:] = comb


def _modulate(x, norm_g, mod, sc_chunk, sh_chunk, tm, router=None, pack_weights=False):
    n, d = x.shape
    row, col = (lambda i: i), (lambda i: 0)
    sc, sc_spec = mod.operand(sc_chunk, tm, d, row, col)
    sh, sh_spec = mod.operand(sh_chunk, tm, d, row, col)
    in_specs = [pl.BlockSpec((tm, d), lambda i: (i, 0)), pl.BlockSpec((1, d), lambda i: (0, 0)), sc_spec, sh_spec]
    args = [x, norm_g.reshape(1, d), sc, sh]
    out_specs = pl.BlockSpec((tm, d), lambda i: (i, 0))
    out_shape = jax.ShapeDtypeStruct((n, d), bf16)
    body = _modulate_body
    if router is not None:
        w_router_pad, b_router_pad = router
        in_specs += [pl.BlockSpec((d, LANES), lambda i: (0, 0)), pl.BlockSpec((1, LANES), lambda i: (0, 0))]
        args += [w_router_pad, b_router_pad]
        width = d + LANES if pack_weights else d
        out_specs = [pl.BlockSpec((tm, width), lambda i: (i, 0)), pl.BlockSpec((tm, LANES), lambda i: (i, 0))]
        out_shape = [jax.ShapeDtypeStruct((n, width), f32 if pack_weights else bf16),
                     jax.ShapeDtypeStruct((n, LANES), f32)]
        body = _modulate_route_body
    return pl.pallas_call(body, grid=(n // tm,), in_specs=in_specs, out_specs=out_specs, out_shape=out_shape,
                          compiler_params=_params(1), name="modulate")(*args)


def _proj_body(epilogue, n_extra, h_ref, w_ref, *rest):
    extra, outs, wb_ref = rest[:n_extra], rest[n_extra:-1], rest[-1]

    @pl.when(pl.program_id(1) == 0)
    def _():
        wb_ref[...] = w_ref[...].T.astype(bf16)

    acc = jnp.dot(h_ref[...], wb_ref[...], preferred_element_type=f32)
    epilogue(acc, extra, outs)


def _heads(width):
    return [slice(h * HEAD_DIM, (h + 1) * HEAD_DIM) for h in range(width // HEAD_DIM)]


def _epi_rope(acc, extra, outs):
    cos, sin = extra[0][...], extra[1][...]
    kscale = jnp.where(pl.program_id(0) == 1, HEAD_DIM ** -0.5, 1.0).astype(f32)
    for hs in _heads(acc.shape[1]):
        a = acc[:, hs]
        outs[0][:, hs] = ((a * cos + pltpu.roll(a, HEAD_DIM // 2, 1) * sin) * kscale).astype(outs[0].dtype)


def _epi_headnorm(acc, extra, outs):
    g, post = extra[0][...], extra[1][...]
    for hs in _heads(acc.shape[1]):
        a = acc[:, hs]
        y = a * lax.rsqrt(jnp.mean(a * a, axis=-1, keepdims=True) + EPS) * g
        for o in outs[:-1]:
            o[:, hs] = y.astype(o.dtype)
        outs[-1][:, hs] = (y * post).astype(outs[-1].dtype)


def _epi_plain(acc, extra, outs):
    for o in outs:
        o[...] = acc.astype(o.dtype)


def _epi_forget(acc, extra, outs):
    y = _log_sigmoid(acc + extra[0][...])
    outs[0][...] = y[:, :outs[0].shape[1]]


def _proj(h16, w_t, layer, col0, ncol, tn, tm, epilogue, extra, out_dtypes, out_width=None):
    n, k = h16.shape
    out_width = tn if out_width is None else out_width
    in_specs = [pl.BlockSpec((tm, k), lambda j, i: (i, 0)),
                pl.BlockSpec((None, tn, k), lambda j, i: (layer, col0 + j, 0))] + [s for _, s in extra]
    out_specs = [pl.BlockSpec((tm, out_width), lambda j, i: (i, j)) for _ in out_dtypes]
    out_shape = [jax.ShapeDtypeStruct((n, ncol * out_width), dt) for dt in out_dtypes]
    return pl.pallas_call(
        functools.partial(_proj_body, epilogue, len(extra)), grid=(ncol, n // tm),
        in_specs=in_specs, out_specs=out_specs, out_shape=out_shape,
        scratch_shapes=[pltpu.VMEM((k, tn), bf16)],
        compiler_params=_params(2), name="proj_" + epilogue.__name__[5:])(h16, w_t, *[a for a, _ in extra])


def _group_norm_gate(o, gain, gate):
    mu = jnp.mean(o, axis=-1, keepdims=True)
    var = jnp.mean(jnp.square(o - mu), axis=-1, keepdims=True)
    return (o - mu) * lax.rsqrt(var + EPS) * gain * _silu(gate)


def _ret_prompt_body(q_ref, k_ref, v_ref, g_ref, intra_ref, crossd_ref, kdec_ref, sdec_ref, gn_ref, o_ref, st_ref):
    @pl.when(pl.program_id(1) == 0)
    def _():
        st_ref[...] = jnp.zeros(st_ref.shape, f32)

    for h, hs in enumerate(_heads(q_ref.shape[1])):
        q, k, v = q_ref[:, hs], k_ref[:, hs], v_ref[:, hs]
        state = st_ref[h]
        scores = lax.dot_general(q, k, _NT, preferred_element_type=f32) * intra_ref[h]
        o = jnp.dot(scores.astype(bf16), v, preferred_element_type=f32)
        o = o + jnp.dot(q, state.astype(bf16), preferred_element_type=f32) * crossd_ref[h]
        k_dec = (k.astype(f32) * kdec_ref[h]).astype(bf16)
        st_ref[h] = state * sdec_ref[h] + lax.dot_general(k_dec, v, _TN, preferred_element_type=f32)
        o_ref[:, hs] = _group_norm_gate(o, gn_ref[:, hs], g_ref[:, hs].astype(f32)).astype(o_ref.dtype)


def _ret_tables():
    lg = jnp.log1p(-jnp.exp2(-5.0 - jnp.arange(N_HEADS, dtype=f32)))[:, None, None]
    t = jnp.arange(RET_CHUNK, dtype=f32)
    diff = t[None, :, None] - t[None, None, :]
    intra = jnp.where(diff >= 0, jnp.exp(lg * jnp.maximum(diff, 0.0)), 0.0)
    ones = jnp.ones((1, 1, HEAD_DIM), f32)
    crossd = jnp.exp(lg * (t[None, :, None] + 1.0)) * ones
    kdec = jnp.exp(lg * (RET_CHUNK - 1.0 - t[None, :, None])) * ones
    sdec = jnp.exp(lg * RET_CHUNK) * ones
    step_dec = jnp.exp(lg) * ones
    return intra, crossd, kdec, sdec, step_dec


def _ret_prompt(qk16, vg16, ret_norm_g, batch, tables):
    n = qk16.shape[0]
    nc = n // batch // RET_CHUNK
    intra, crossd, kdec, sdec, _ = tables
    row = lambda b, c: b * nc + c
    full3 = lambda a: pl.BlockSpec(a.shape, lambda b, c: (0, 0, 0))
    blk = lambda col: pl.BlockSpec((RET_CHUNK, GROUP_WIDTH), lambda b, c: (row(b, c), col))
    return pl.pallas_call(
        _ret_prompt_body, grid=(batch, nc),
        in_specs=[blk(0), blk(1), blk(0), blk(1), full3(intra), full3(crossd), full3(kdec), full3(sdec),
                  pl.BlockSpec((1, GROUP_WIDTH), lambda b, c: (0, 0))],
        out_specs=[blk(0), pl.BlockSpec((None, N_HEADS, HEAD_DIM, HEAD_DIM), lambda b, c: (b, 0, 0, 0))],
        out_shape=[jax.ShapeDtypeStruct((n, GROUP_WIDTH), bf16),
                   jax.ShapeDtypeStruct((batch, N_HEADS, HEAD_DIM, HEAD_DIM), f32)],
        compiler_params=_params(2), name="ret_prompt")(
            qk16, qk16, vg16, vg16, intra, crossd, kdec, sdec, ret_norm_g.reshape(1, GROUP_WIDTH))


def _block_diag(x):
    xt = jnp.tile(x, (1, N_HEADS))
    lane = lax.broadcasted_iota(jnp.int32, xt.shape, 1)
    row = lax.broadcasted_iota(jnp.int32, xt.shape, 0)
    return jnp.where((lane >> 7) == row, xt, 0.0)


RET_SAMPLE_SEQS = 4


def _ret_sample_body(q_ref, k_ref, v_ref, g_ref, st_ref, dec3_ref, dec2_ref, gn_ref, o_ref, sn_ref):
    for i in range(q_ref.shape[0]):
        q16, k16, v16 = q_ref[i].astype(bf16), k_ref[i].astype(bf16), v_ref[i].astype(bf16)
        q, k, v = q16.astype(f32), k16.astype(f32), v16.astype(f32)
        state = st_ref[i]
        cross = jnp.dot(_block_diag(q).astype(bf16), state.reshape(GROUP_WIDTH, HEAD_DIM).astype(bf16),
                        preferred_element_type=f32) * dec2_ref[...]
        qk = jnp.sum(q * k, axis=-1, keepdims=True).astype(bf16).astype(f32)
        o = qk * v + cross
        upd = lax.dot_general(_block_diag(k).astype(bf16), v16, _TN, preferred_element_type=f32)
        sn_ref[i] = state * dec3_ref[...] + upd.reshape(state.shape)
        o_ref[i] = _group_norm_gate(o, gn_ref[...], g_ref[i]).astype(o_ref.dtype)


def _ret_sample(q3, k3, v3, g3, state_ret, layer, ret_norm_g, step_dec):
    bd = q3.shape[0]
    nb = RET_SAMPLE_SEQS if bd % RET_SAMPLE_SEQS == 0 else 1
    tok = pl.BlockSpec((nb, N_HEADS, HEAD_DIM), lambda b: (b, 0, 0))
    return pl.pallas_call(
        _ret_sample_body, grid=(bd // nb,),
        in_specs=[tok, tok, tok, tok,
                  pl.BlockSpec((None, nb, N_HEADS, HEAD_DIM, HEAD_DIM), lambda b: (layer, b, 0, 0, 0)),
                  pl.BlockSpec(step_dec.shape, lambda b: (0, 0, 0)),
                  pl.BlockSpec((N_HEADS, HEAD_DIM), lambda b: (0, 0)),
                  pl.BlockSpec((N_HEADS, HEAD_DIM), lambda b: (0, 0))],
        out_specs=[tok, pl.BlockSpec((nb, N_HEADS, HEAD_DIM, HEAD_DIM), lambda b: (b, 0, 0, 0))],
        out_shape=[jax.ShapeDtypeStruct((bd, N_HEADS, HEAD_DIM), f32),
                   jax.ShapeDtypeStruct((bd, N_HEADS, HEAD_DIM, HEAD_DIM), f32)],
        compiler_params=_params(1), name="ret_sample")(
            q3, k3, v3, g3, state_ret, step_dec, step_dec.reshape(N_HEADS, HEAD_DIM),
            ret_norm_g.reshape(N_HEADS, HEAD_DIM))


def _cumsum_body(x_ref, o_ref, carry_ref):
    @pl.when(pl.program_id(1) == 0)
    def _():
        carry_ref[...] = jnp.zeros(carry_ref.shape, f32)

    tb = x_ref.shape[1]
    upper = (lax.broadcasted_iota(jnp.int32, (tb, tb), 0) <= lax.broadcasted_iota(jnp.int32, (tb, tb), 1)).astype(f32)
    cum = jnp.dot(x_ref[...], upper, precision=lax.Precision.HIGHEST, preferred_element_type=f32) + carry_ref[:, :1]
    o_ref[...] = cum
    carry_ref[...] = jnp.broadcast_to(cum[:, tb - 1:tb], carry_ref.shape)


def _cumsum_lanes(x_t, tb):
    b, h, s = x_t.shape
    spec = pl.BlockSpec((None, h, tb), lambda i, j: (i, 0, j))
    return pl.pallas_call(_cumsum_body, grid=(b, s // tb), in_specs=[spec], out_specs=spec,
                          out_shape=jax.ShapeDtypeStruct(x_t.shape, f32),
                          scratch_shapes=[pltpu.VMEM((h, LANES), f32)],
                          compiler_params=_params(2), name="forget_cumsum")(x_t)


LOG2E = 1.4426950408889634
HEADS_PER_STEP = 2


def _forget_tail(f_sel, first_lane, sign, ones_lane):
    x = f_sel * (sign * LOG2E)
    hi = x.astype(bf16).astype(f32)
    rest = x - hi
    mid = rest.astype(bf16).astype(f32)
    lo = (rest - mid).astype(bf16).astype(f32)
    lane = lax.broadcasted_iota(jnp.int32, (f_sel.shape[0], HEAD_DIM), 1)
    ones = jnp.where((lane >= ones_lane) & (lane < ones_lane + 3), 1.0, 0.0)
    tail = jnp.where(lane == first_lane, hi, jnp.where(lane == first_lane + 1, mid,
                                                      jnp.where(lane == first_lane + 2, lo, ones)))
    return tail.astype(bf16)


def _select_head(f_block, head):
    lane = lax.broadcasted_iota(jnp.int32, f_block.shape, 1)
    return jnp.sum(jnp.where(lane == head, f_block, 0.0), axis=-1, keepdims=True)


def _fox_prompt_body(q_ref, k_ref, v_ref, fq_ref, fk_ref, o_ref, qaug_ref, kaug_ref, m_ref, l_ref, acc_ref):
    pair, qi = pl.program_id(1), pl.program_id(2)
    tq = q_ref.shape[0]
    n_sub = q_ref.shape[1] // HEAD_DIM
    seq = k_ref.shape[0]

    @pl.when(qi == 0)
    def _():
        def fill(c, carry):
            rows = pl.ds(pl.multiple_of(c * tq, tq), tq)
            for hh, hs in enumerate(_heads(q_ref.shape[1])):
                kaug_ref[hh, rows, :HEAD_DIM] = k_ref[rows, hs]
                kaug_ref[hh, rows, HEAD_DIM:] = _forget_tail(_select_head(fk_ref[rows, :], pair * n_sub + hh), 3, -1.0, 0)
            return carry
        lax.fori_loop(0, seq // tq, fill, 0)

    for hh, hs in enumerate(_heads(q_ref.shape[1])):
        qaug_ref[hh, :, :HEAD_DIM] = q_ref[:, hs]
        qaug_ref[hh, :, HEAD_DIM:] = _forget_tail(_select_head(fq_ref[...], pair * n_sub + hh), 0, 1.0, 3)
    m_ref[...] = jnp.full(m_ref.shape, NEG_BIG, f32)
    l_ref[...] = jnp.zeros(l_ref.shape, f32)
    acc_ref[...] = jnp.zeros(acc_ref.shape, f32)

    def step(j, diagonal):
        rows = pl.ds(pl.multiple_of(j * tq, tq), tq)
        for hh, hs in enumerate(_heads(q_ref.shape[1])):
            s = lax.dot_general(qaug_ref[hh], kaug_ref[hh, rows, :], _NT, preferred_element_type=f32)
            if diagonal:
                r = lax.broadcasted_iota(jnp.int32, s.shape, 0)
                c = lax.broadcasted_iota(jnp.int32, s.shape, 1)
                s = jnp.where(c <= r, s, -jnp.inf)
            m_old = m_ref[hh]
            m_new = jnp.maximum(m_old, jnp.max(s, axis=-1, keepdims=True))
            alpha = jnp.exp2(m_old - m_new)
            p = jnp.exp2(s - jnp.tile(m_new, (1, tq // LANES)))
            l_ref[hh] = alpha * l_ref[hh] + jnp.sum(p, axis=-1, keepdims=True)
            acc_ref[hh] = alpha * acc_ref[hh] + jnp.dot(p.astype(bf16), v_ref[rows, hs], preferred_element_type=f32)
            m_ref[hh] = m_new

    def off_diagonal(j, carry):
        step(j, False)
        return carry

    lax.fori_loop(0, qi, off_diagonal, 0)
    step(qi, True)
    for hh, hs in enumerate(_heads(q_ref.shape[1])):
        o_ref[:, hs] = (acc_ref[hh] / l_ref[hh]).astype(o_ref.dtype)


def _fox_prompt(q16, k16, v16, f_cols, batch, tq):
    n = q16.shape[0]
    s = n // batch
    nq = s // tq
    width = HEADS_PER_STEP * HEAD_DIM
    kv = lambda a: a.reshape(batch, s, GROUP_WIDTH)
    kv_spec = pl.BlockSpec((None, s, width), lambda b, h, i: (b, 0, h))
    q_spec = pl.BlockSpec((tq, width), lambda b, h, i: (b * nq + i, h))
    sub = (HEADS_PER_STEP, tq, HEAD_DIM)
    return pl.pallas_call(
        _fox_prompt_body, grid=(batch, N_HEADS // HEADS_PER_STEP, nq),
        in_specs=[q_spec, kv_spec, kv_spec,
                  pl.BlockSpec((None, tq, N_HEADS), lambda b, h, i: (b, i, 0)),
                  pl.BlockSpec((None, s, N_HEADS), lambda b, h, i: (b, 0, 0))],
        out_specs=q_spec,
        out_shape=jax.ShapeDtypeStruct((n, GROUP_WIDTH), bf16),
        scratch_shapes=[pltpu.VMEM((HEADS_PER_STEP, tq, 2 * HEAD_DIM), bf16),
                        pltpu.VMEM((HEADS_PER_STEP, s, 2 * HEAD_DIM), bf16),
                        pltpu.VMEM(sub, f32), pltpu.VMEM(sub, f32), pltpu.VMEM(sub, f32)],
        compiler_params=_params(3), name="fox_prompt")(q16, kv(k16), kv(v16), f_cols, f_cols)


FLAT = PAGE_SIZE * N_HEADS
PAGES_PER_STEP = 16


def _dot_select(y, m01):
    hi = y.astype(bf16)
    rest = y - hi.astype(f32)
    mid = rest.astype(bf16)
    lo = (rest - mid.astype(f32)).astype(bf16)
    return sum(jnp.dot(piece, m01, preferred_element_type=f32) for piece in (hi, mid, lo))


def _forget_pages_body(x_ref, excl_ref, tot_ref):
    x = x_ref[...]
    t = x.shape[1]
    later = (lax.broadcasted_iota(jnp.int32, (t, t), 0) > lax.broadcasted_iota(jnp.int32, (t, t), 1)).astype(bf16)
    excl_ref[...] = _dot_select(x, later)
    tot_ref[...] = jnp.broadcast_to(jnp.sum(x, axis=-1, keepdims=True), x.shape)


def _forget_pages(cache_logf):
    depth, n_phys, page, heads = cache_logf.shape
    assert page == PAGE_SIZE and heads == N_HEADS
    n = depth * n_phys
    assert n % SUBLANES == 0
    nr = n * heads
    tr = next((c for c in (2048, 1024, 512, 256, 128, 64) if nr % c == 0), nr)
    rows = cache_logf.transpose(0, 1, 3, 2).reshape(nr, page)
    spec = pl.BlockSpec((tr, page), lambda i: (i, 0))
    out = jax.ShapeDtypeStruct((nr, page), f32)
    excl, tot = pl.pallas_call(_forget_pages_body, grid=(nr // tr,), in_specs=[spec], out_specs=[spec, spec],
                               out_shape=[out, out], compiler_params=_params(1), name="forget_pages")(rows)
    excl_flat = excl.reshape(n, heads, page).transpose(0, 2, 1).reshape(n, FLAT)
    tot_flat = jnp.tile(tot[:, 0].reshape(n, heads), (1, page))
    return excl_flat, tot_flat


def _fox_sample_body(n_group, page_row0, pt_ref, q_ref, kn_ref, vn_ref, lnew_ref, *rest):
    k_refs, v_refs = rest[:n_group], rest[n_group:2 * n_group]
    excl_refs, tot_refs = rest[2 * n_group:3 * n_group], rest[3 * n_group:4 * n_group]
    o_ref, carry_ref, m_ref, l_ref, acc_ref = rest[4 * n_group:]
    step = pl.program_id(1)
    n_pages = pl.num_programs(1) * n_group
    scale = HEAD_DIM ** -0.5

    @pl.when(step == 0)
    def _():
        carry_ref[...] = lnew_ref[...]
        m_ref[...] = jnp.full(m_ref.shape, NEG_BIG, f32)
        l_ref[...] = jnp.zeros(l_ref.shape, f32)
        acc_ref[...] = jnp.zeros(acc_ref.shape, f32)

    q16 = q_ref[...].astype(bf16)
    own_head = ((lax.broadcasted_iota(jnp.int32, (N_HEADS, FLAT), 1) & (N_HEADS - 1))
                == lax.broadcasted_iota(jnp.int32, (N_HEADS, FLAT), 0))
    carry = carry_ref[...]
    scores = []
    for j in range(n_group):
        kf = k_refs[j][...].reshape(FLAT, HEAD_DIM).astype(bf16)
        row = pl.ds((page_row0 + pt_ref[pl.program_id(0), n_pages - 1 - (step * n_group + j)]) % SUBLANES, 1)
        s = lax.dot_general(q16, kf, _NT, preferred_element_type=f32) * scale + (carry + excl_refs[j][row, :])
        scores.append(jnp.where(own_head, s, -jnp.inf))
        carry = carry + tot_refs[j][row, :]
    carry_ref[...] = carry

    top = scores[0]
    for s in scores[1:]:
        top = jnp.maximum(top, s)
    m_old = m_ref[...]
    m_new = jnp.maximum(m_old, jnp.max(top, axis=-1, keepdims=True))
    alpha = jnp.exp(m_old - m_new)
    l_new = alpha * l_ref[...]
    acc = alpha * acc_ref[...]
    for j in range(n_group):
        pe = jnp.exp(scores[j] - m_new)
        l_new = l_new + jnp.sum(pe, axis=-1, keepdims=True)
        vf = v_refs[j][...].reshape(FLAT, HEAD_DIM).astype(bf16)
        acc = acc + jnp.dot(pe.astype(bf16), vf, preferred_element_type=f32)
    m_ref[...] = m_new
    l_ref[...] = l_new
    acc_ref[...] = acc

    @pl.when(step == pl.num_programs(1) - 1)
    def _():
        rounded = lambda ref: ref[...].astype(bf16).astype(f32)
        s_new = jnp.sum(rounded(q_ref) * rounded(kn_ref), axis=-1, keepdims=True) * scale
        m_fin = jnp.maximum(m_new, s_new)
        beta = jnp.exp(m_new - m_fin)
        p_new = jnp.exp(s_new - m_fin)
        num = beta * acc + p_new.astype(bf16).astype(f32) * rounded(vn_ref)
        o_ref[...] = (num / (beta * l_new + p_new)).astype(o_ref.dtype)


def _fox_sample(q3, kn3, vn3, logf_new, cache_k, cache_v, excl_flat, tot_flat, layer, page_table):
    bd, n_pages = page_table.shape
    n_phys = cache_k.shape[1]
    g = PAGES_PER_STEP if n_pages % PAGES_PER_STEP == 0 else 1
    lnew_flat = jnp.tile(logf_new, (1, PAGE_SIZE)).reshape(bd, 1, FLAT)
    tok = pl.BlockSpec((None, N_HEADS, HEAD_DIM), lambda b, s, pt: (b, 0, 0))

    def page_of(j):
        return lambda b, s, pt: pt[b, n_pages - 1 - (s * g + j)]

    def kv_spec(j):
        pg = page_of(j)
        return pl.BlockSpec((None, None, PAGE_SIZE, N_HEADS, HEAD_DIM), lambda b, s, pt: (layer, pg(b, s, pt), 0, 0, 0))

    def flat_spec(j):
        pg = page_of(j)
        return pl.BlockSpec((SUBLANES, FLAT), lambda b, s, pt: ((layer * n_phys + pg(b, s, pt)) // SUBLANES, 0))

    groups = [[kv_spec(j) for j in range(g)], [kv_spec(j) for j in range(g)],
              [flat_spec(j) for j in range(g)], [flat_spec(j) for j in range(g)]]
    grid_spec = pltpu.PrefetchScalarGridSpec(
        num_scalar_prefetch=1, grid=(bd, n_pages // g),
        in_specs=[tok, tok, tok, pl.BlockSpec((None, 1, FLAT), lambda b, s, pt: (b, 0, 0))] + sum(groups, []),
        out_specs=tok,
        scratch_shapes=[pltpu.VMEM((1, FLAT), f32), pltpu.VMEM((N_HEADS, 1), f32), pltpu.VMEM((N_HEADS, 1), f32),
                        pltpu.VMEM((N_HEADS, HEAD_DIM), f32)])
    return pl.pallas_call(
        functools.partial(_fox_sample_body, g, layer * n_phys), grid_spec=grid_spec,
        out_shape=jax.ShapeDtypeStruct((bd, N_HEADS, HEAD_DIM), f32),
        compiler_params=_params(2), name="fox_sample")(
            page_table, q3, kn3, vn3, lnew_flat, *([cache_k] * g + [cache_v] * g + [excl_flat] * g + [tot_flat] * g))


def _out_proj_body(ret_ref, fox_ref, w_ref, x_ref, ga_ref, o_ref, wb_ref):
    @pl.when(pl.program_id(1) == 0)
    def _():
        wb_ref[...] = w_ref[...].astype(bf16)

    half = ret_ref.shape[1]
    y = jnp.dot(ret_ref[...].astype(bf16), wb_ref[:half, :], preferred_element_type=f32)
    y = y + jnp.dot(fox_ref[...].astype(bf16), wb_ref[half:, :], preferred_element_type=f32)
    o_ref[...] = x_ref[...] + ga_ref[...] * y


def _out_proj(ret16, fox16, w_out, layer, x, mod, tm, tn):
    n, d = x.shape
    k = w_out.shape[1]
    ga, ga_spec = mod.operand(2, tm, tn, lambda j, i: i, lambda j, i: j)
    act = pl.BlockSpec((tm, ret16.shape[1]), lambda j, i: (i, 0))
    xs = pl.BlockSpec((tm, tn), lambda j, i: (i, j))
    return pl.pallas_call(
        _out_proj_body, grid=(d // tn, n // tm),
        in_specs=[act, act, pl.BlockSpec((None, k, tn), lambda j, i: (layer, 0, j)), xs, ga_spec],
        out_specs=xs, out_shape=jax.ShapeDtypeStruct((n, d), f32),
        scratch_shapes=[pltpu.VMEM((k, tn), bf16)],
        compiler_params=_params(2), name="out_proj")(ret16, fox16, w_out, x, ga)


def _expert(h, wg_ref, wu_ref, wd_ref):
    g = jnp.dot(h, wg_ref[...], preferred_element_type=f32)
    u = jnp.dot(h, wu_ref[...], preferred_element_type=f32)
    return jnp.dot(_silu(g) * u, wd_ref[...], preferred_element_type=f32)


def _moe_body(h_ref, comb_ref, wg_ref, wu_ref, wd_ref, x_ref, gm_ref, o_ref, acc_ref):
    e = pl.program_id(1)

    @pl.when(e == 0)
    def _():
        acc_ref[...] = jnp.zeros(acc_ref.shape, f32)

    y = _expert(h_ref[...].astype(f32), wg_ref, wu_ref, wd_ref)
    lane = lax.broadcasted_iota(jnp.int32, comb_ref.shape, 1)
    w = jnp.sum(jnp.where(lane == e, comb_ref[...], 0.0), axis=-1, keepdims=True)
    acc_ref[...] += w * y

    @pl.when(e == pl.num_programs(1) - 1)
    def _():
        o_ref[...] = x_ref[...] + gm_ref[...] * acc_ref[...]


def _moe(h16, comb, wg, wu, wd, layer, x, mod, tm):
    n, d = x.shape
    n_exp, _, de = wg.shape[1:]
    gm, gm_spec = mod.operand(5, tm, d, lambda i, e: i, lambda i, e: 0)
    rows = lambda w: pl.BlockSpec((tm, w), lambda i, e: (i, 0))
    return pl.pallas_call(
        _moe_body, grid=(n // tm, n_exp),
        in_specs=[rows(d), rows(LANES),
                  pl.BlockSpec((None, None, d, de), lambda i, e: (layer, e, 0, 0)),
                  pl.BlockSpec((None, None, d, de), lambda i, e: (layer, e, 0, 0)),
                  pl.BlockSpec((None, None, de, d), lambda i, e: (layer, e, 0, 0)),
                  rows(d), gm_spec],
        out_specs=rows(d), out_shape=jax.ShapeDtypeStruct((n, d), f32),
        scratch_shapes=[pltpu.VMEM((tm, d), f32)],
        compiler_params=_params(2), name="moe")(h16, comb, wg, wu, wd, x, gm)


N_GROUPS = N_EXPERTS // EXPERTS_PER_GROUP
ROW_DMA_UNROLL = 8
ROW_DMA_PRIORITY = 1


def _group_layout(group, tile):
    n = group.shape[0]
    onehot = (group[:, None] == jnp.arange(N_GROUPS, dtype=jnp.int32)[None, :]).astype(jnp.int32)
    running = jnp.cumsum(onehot, axis=0)
    rank = jnp.sum(onehot * running, axis=1) - 1
    padded = (running[-1] + tile - 1) // tile * tile
    ends = jnp.cumsum(padded)
    slot = jnp.sum(onehot * (ends - padded)[None, :], axis=1) + rank
    starts = jnp.arange(n // tile + N_GROUPS, dtype=jnp.int32) * tile
    tile_group = jnp.minimum(jnp.sum((starts[:, None] >= ends[None, :]).astype(jnp.int32), axis=1), N_GROUPS - 1)
    return slot.astype(jnp.int32), tile_group.astype(jnp.int32), (ends[-1] // tile).astype(jnp.int32).reshape(1)


def _start_rows(n_rows, copy_of):
    def start(c, carry):
        for u in range(ROW_DMA_UNROLL):
            copy_of(c * ROW_DMA_UNROLL + u).start(priority=ROW_DMA_PRIORITY)
        return carry

    lax.fori_loop(0, n_rows // ROW_DMA_UNROLL, start, 0)


def _wait_rows(n_rows, copy_of):
    def wait(c, carry):
        for u in range(ROW_DMA_UNROLL):
            copy_of(c * ROW_DMA_UNROLL + u).wait()
        return carry

    lax.fori_loop(0, n_rows // ROW_DMA_UNROLL, wait, 0)


def _snake(tile, step):
    return jnp.where(tile % 2 == 0, step, EXPERTS_PER_GROUP - 1 - step)


def _moe_group_body(tile_rows, slot_ref, tg_ref, used_ref, hx_hbm, wg_ref, wu_ref, wd_ref, o_ref,
                    rows_ref, source_ref, sems):
    tile, e4 = pl.program_id(0), pl.program_id(1)
    d = wg_ref.shape[0]
    used = used_ref[0]

    def gather(t):
        return lambda k: pltpu.make_async_copy(hx_hbm.at[pl.ds(source_ref[t * tile_rows + k], 1)],
                                               rows_ref.at[t % 2, pl.ds(k, 1)], sems.at[t % 2])

    @pl.when((tile == 0) & (e4 == 0))
    def _():
        def clear(c, carry):
            for u in range(ROW_DMA_UNROLL):
                source_ref[c * ROW_DMA_UNROLL + u] = 0
            return carry

        def place(c, carry):
            for u in range(ROW_DMA_UNROLL):
                source_ref[slot_ref[c * ROW_DMA_UNROLL + u]] = c * ROW_DMA_UNROLL + u
            return carry

        lax.fori_loop(0, source_ref.shape[0] // ROW_DMA_UNROLL, clear, 0)
        lax.fori_loop(0, slot_ref.shape[0] // ROW_DMA_UNROLL, place, 0)
        _start_rows(tile_rows, gather(0))

    @pl.when((e4 == 1) & (tile + 1 < used))
    def _():
        _start_rows(tile_rows, gather(tile + 1))

    @pl.when(e4 == 0)
    def _():
        o_ref[...] = jnp.zeros(o_ref.shape, f32)

    @pl.when((e4 == 0) & (tile < used))
    def _():
        _wait_rows(tile_rows, gather(tile))

    @pl.when(tile < used)
    def _():
        rows = rows_ref.at[tile % 2]
        y = _expert(rows[:, :d], wg_ref, wu_ref, wd_ref)
        e = tg_ref[tile] * EXPERTS_PER_GROUP + _snake(tile, e4)
        comb = rows[:, d:]
        lane = lax.broadcasted_iota(jnp.int32, comb.shape, 1)
        o_ref[...] += jnp.sum(jnp.where(lane == e, comb, 0.0), axis=-1, keepdims=True) * y


def _moe_group(hx, slot, tile_group, n_used, wg, wu, wd, layer, tile):
    n_slots = hx.shape[0] + N_GROUPS * tile
    d, de = wg.shape[2:]
    assert hx.shape[0] % ROW_DMA_UNROLL == 0 and tile % ROW_DMA_UNROLL == 0

    def expert(i, e4, sl, tg, used):
        last = used[0] - 1
        return jnp.where(i <= last, tg[i] * EXPERTS_PER_GROUP + _snake(i, e4),
                         tg[last] * EXPERTS_PER_GROUP + _snake(last, EXPERTS_PER_GROUP - 1))

    weights = lambda rows, cols: pl.BlockSpec((None, None, rows, cols),
                                              lambda i, e4, sl, tg, used: (layer, expert(i, e4, sl, tg, used), 0, 0))
    grid_spec = pltpu.PrefetchScalarGridSpec(
        num_scalar_prefetch=3, grid=(n_slots // tile, EXPERTS_PER_GROUP),
        in_specs=[pl.BlockSpec(memory_space=pl.ANY), weights(d, de), weights(d, de), weights(de, d)],
        out_specs=pl.BlockSpec((tile, d), lambda i, e4, sl, tg, used: (i, 0)),
        scratch_shapes=[pltpu.VMEM((2, tile, hx.shape[1]), f32), pltpu.SMEM((n_slots,), jnp.int32),
                        pltpu.SemaphoreType.DMA((2,))])
    return pl.pallas_call(functools.partial(_moe_group_body, tile), grid_spec=grid_spec,
                          out_shape=jax.ShapeDtypeStruct((n_slots, d), f32),
                          compiler_params=_params(2), name="moe_group")(slot, tile_group, n_used, hx, wg, wu, wd)


def _moe_combine_body(slot_ref, x_ref, gm_ref, ys_hbm, o_ref, rows_ref, sem):
    tm = x_ref.shape[0]
    base = pl.program_id(0) * tm
    fetch = lambda k: pltpu.make_async_copy(ys_hbm.at[pl.ds(slot_ref[base + k], 1)], rows_ref.at[pl.ds(k, 1)], sem)
    _start_rows(tm, fetch)
    _wait_rows(tm, fetch)
    o_ref[...] = x_ref[...] + gm_ref[...] * rows_ref[...]


def _moe_combine(x, mod, ys, slot, tm):
    n, d = x.shape
    gm, gm_spec = mod.operand(5, tm, d, lambda i, s: i, lambda i, s: 0)
    rows = pl.BlockSpec((tm, d), lambda i, s: (i, 0))
    grid_spec = pltpu.PrefetchScalarGridSpec(
        num_scalar_prefetch=1, grid=(n // tm,),
        in_specs=[rows, gm_spec, pl.BlockSpec(memory_space=pl.ANY)], out_specs=rows,
        scratch_shapes=[pltpu.VMEM((tm, d), f32), pltpu.SemaphoreType.DMA(())])
    return pl.pallas_call(_moe_combine_body, grid_spec=grid_spec, out_shape=jax.ShapeDtypeStruct((n, d), f32),
                          compiler_params=_params(1), name="moe_combine")(slot, x, gm, ys)


def _rope_tables(positions):
    half = HEAD_DIM // 2
    inv_freq = ROPE_BASE ** (-jnp.arange(half, dtype=f32) / half)
    ang = positions.astype(f32)[:, None] * inv_freq[None, :]
    cos, sin = jnp.cos(ang), jnp.sin(ang)
    return jnp.concatenate([cos, cos], axis=-1), jnp.concatenate([-sin, sin], axis=-1)


def _mixer_in(x, mod, layer, tm, pos_tables, pos_tiles, p, lo, q_scale):
    h16 = _modulate(x, p["norm_mix_g"][layer], mod, 1, 0, tm)
    cos, sin = pos_tables
    pos_spec = pl.BlockSpec((tm, HEAD_DIM), lambda j, i: (i % pos_tiles, 0))
    w_in = p["w_in_t"]
    gw = GROUP_WIDTH
    (qk16,) = _proj(h16, w_in, layer, 0, 2, gw, tm, _epi_rope, [(cos, pos_spec), (sin, pos_spec)], [lo])
    (vg16,) = _proj(h16, w_in, layer, 2, 2, gw, tm, _epi_plain, [], [lo])
    row = lambda a: (a.reshape(1, HEAD_DIM), pl.BlockSpec((1, HEAD_DIM), lambda j, i: (0, 0)))
    const = lambda c: row(jnp.full((HEAD_DIM,), c, f32))
    (fq16,) = _proj(h16, w_in, layer, 4, 1, gw, tm, _epi_headnorm, [row(p["q_norm_g"][layer]), const(q_scale)], [lo])
    fk32, fk16 = _proj(h16, w_in, layer, 5, 1, gw, tm, _epi_headnorm, [row(p["k_norm_g"][layer]), const(1.0)], [f32, lo])
    fv32, fv16 = _proj(h16, w_in, layer, 6, 1, gw, tm, _epi_plain, [], [f32, lo])
    n_forget = p["b_forget"].shape[1]
    w_f = jnp.pad(w_in[layer, 7 * gw:, :], ((0, LANES - n_forget), (0, 0)))[None]
    b_f = jnp.pad(p["b_forget"][layer], (0, LANES - n_forget)).reshape(1, LANES)
    (logf,) = _proj(h16, w_f, 0, 0, 1, LANES, tm, _epi_forget,
                    [(b_f, pl.BlockSpec((1, LANES), lambda j, i: (0, 0)))], [f32], out_width=n_forget)
    return qk16, vg16, fq16, fk32, fk16, fv32, fv16, logf


def _channel(x, mod, layer, tm, p, grouped):
    router = (p["w_router_pad"], p["b_router_pad"])
    experts = (p["w_gate"], p["w_up"], p["w_down"])
    if not grouped:
        h16, comb = _modulate(x, p["norm_ffn_g"][layer], mod, 4, 3, tm, router=router)
        return _moe(h16, comb, *experts, layer, x, mod, tm)
    hx, comb = _modulate(x, p["norm_ffn_g"][layer], mod, 4, 3, tm, router=router, pack_weights=True)
    slot, tile_group, n_used = _group_layout(comb[:, N_EXPERTS].astype(jnp.int32), tm)
    ys = _moe_group(hx, slot, tile_group, n_used, *experts, layer, tm)
    return _moe_combine(x, mod, ys, slot, tm)


def kernel(x_prompt, x_sample, cache_k, cache_v, cache_logf, state_ret, page_table, c_prompt, c_sample, w_ada, b_ada,
           norm_mix_g, norm_ffn_g, w_in, b_forget, q_norm_g, k_norm_g, ret_norm_g, w_out, w_router, b_router,
           w_gate, w_up, w_down):
    batch, seq, d = x_prompt.shape
    bd, t_new, _ = x_sample.shape
    assert t_new == 1, "one new token per sampled sequence"
    assert bd % 8 == 0 and d == 2 * GROUP_WIDTH
    depth = w_in.shape[0]
    n_pages = page_table.shape[1]
    past_len = n_pages * PAGE_SIZE
    n_p = batch * seq
    tm_p = min(512, seq)
    tq = min(512, seq)

    p = dict(norm_mix_g=norm_mix_g, norm_ffn_g=norm_ffn_g, w_in_t=w_in.transpose(0, 2, 1), b_forget=b_forget,
             q_norm_g=q_norm_g, k_norm_g=k_norm_g,
             w_router_pad=jnp.pad(w_router, ((0, 0), (0, LANES - w_router.shape[1]))),
             b_router_pad=jnp.pad(b_router, (0, LANES - b_router.shape[0])).reshape(1, LANES),
             w_gate=w_gate, w_up=w_up, w_down=w_down)

    r_rows = -(-(bd + batch) // 8) * 8
    c_all = jnp.concatenate([c_sample, c_prompt, jnp.zeros((r_rows - bd - batch, d), f32)], axis=0)
    mod_all = _adaln(c_all, w_ada, b_ada)

    excl_flat, tot_flat = _forget_pages(cache_logf)
    tables = _ret_tables()
    rope_p = _rope_tables(jnp.arange(seq))
    rope_s = _rope_tables(jnp.full((bd,), past_len))

    xp = x_prompt.reshape(n_p, d)
    xs = x_sample.reshape(bd, d)
    outs = [[] for _ in range(8)]
    tok3 = lambda a: a.reshape(bd, N_HEADS, HEAD_DIM)
    for l in range(depth):
        mod = _Mod(mod_all, l, bd, seq)
        qk16, vg16, fq16, fk32, fk16, fv32, fv16, logf = _mixer_in(xp, mod, l, tm_p, rope_p, seq // tm_p, p, bf16,
                                                                       LOG2E * HEAD_DIM ** -0.5)
        ret16, s_fin = _ret_prompt(qk16, vg16, ret_norm_g[l], batch, tables)
        logf_b = logf.reshape(batch, seq, -1)
        f_rows = _cumsum_lanes(logf_b.transpose(0, 2, 1), tq)
        fox16 = _fox_prompt(fq16, fk16, fv16, f_rows.transpose(0, 2, 1), batch, tq)
        xp = _out_proj(ret16, fox16, w_out, l, xp, mod, tm_p, GROUP_WIDTH)
        xp = _channel(xp, mod, l, tm_p, p, grouped=True)
        outs[0].append(fk32.reshape(batch, seq, N_HEADS, HEAD_DIM))
        outs[1].append(fv32.reshape(batch, seq, N_HEADS, HEAD_DIM))
        outs[2].append(logf_b)
        outs[3].append(s_fin)
        mod = _Mod(mod_all, l, bd, None)
        qk16, vg16, fq16, fk32, fk16, fv32, fv16, logf = _mixer_in(xs, mod, l, bd, rope_s, 1, p, f32, 1.0)
        ret3, s_new = _ret_sample(tok3(qk16[:, :GROUP_WIDTH]), tok3(qk16[:, GROUP_WIDTH:]), tok3(vg16[:, :GROUP_WIDTH]),
                                  tok3(vg16[:, GROUP_WIDTH:]), state_ret, l, ret_norm_g[l], tables[4])
        fox3 = _fox_sample(tok3(fq16), tok3(fk16), tok3(fv16), logf, cache_k, cache_v, excl_flat, tot_flat, l, page_table)
        xs = _out_proj(ret3.reshape(bd, GROUP_WIDTH), fox3.reshape(bd, GROUP_WIDTH), w_out, l, xs, mod, bd, GROUP_WIDTH)
        xs = _channel(xs, mod, l, bd, p, grouped=False)
        outs[4].append(fk32.reshape(bd, 1, N_HEADS, HEAD_DIM))
        outs[5].append(fv32.reshape(bd, 1, N_HEADS, HEAD_DIM))
        outs[6].append(logf.reshape(bd, 1, -1))
        outs[7].append(s_new)
    return (xp.reshape(batch, seq, d), xs.reshape(bd, 1, d)) + tuple(jnp.stack(o) for o in outs)
```
